```python
import math
import jax, jax.numpy as jnp
from jax import lax
import numpy as np

D_MODEL = 1024
BATCH = 8
SEQ = 2048
DEPTH = 4
DEC_BATCH = 32
DEC_SEQ = 8
PAST_LEN = 8192
PAGE_SIZE = 128

D_RNN = 1024
N_RNN_BLOCKS = 16
RNN_BLOCK = D_RNN // N_RNN_BLOCKS
CONV_W = 4
LRU_C = 8.0
H_DIFF = 4
DK_DIFF = 64
DV_DIFF = 2 * DK_DIFF
H_DSA = 8
DH_DSA = 64
H_IDX = 8
D_IDX = 64
TOPK_MAX = 256
ROPE_THETA = 500000.0
D_FF = 2816
FFN_CONV_W = 3
EPS = 1e-6
Q_BLOCK = 128
DSA_Q_BLOCK = 32

W_DIFF = H_DIFF * DV_DIFF
W_DSA = H_DSA * DH_DSA
MIX_WIDTH = D_RNN + W_DIFF + W_DSA
IN_SIZES = (D_RNN, D_RNN,
            H_DIFF * 2 * DK_DIFF, H_DIFF * 2 * DK_DIFF, H_DIFF * DV_DIFF,
            H_DSA * DH_DSA, H_DSA * DH_DSA, H_DSA * DH_DSA,
            H_IDX * D_IDX, D_IDX, H_IDX,
            3 * D_MODEL)
IN_WIDTH = (2 * D_RNN + 4 * H_DIFF * DK_DIFF + H_DIFF * DV_DIFF + 3 * H_DSA * DH_DSA
            + H_IDX * D_IDX + D_IDX + H_IDX + 3 * D_MODEL)

kernel_name = 'hybrid_lru_diffattn_dsa_convffn_step'

F32 = jnp.float32


def _split(z, sizes):
    out, start = [], 0
    for s in sizes:
        out.append(z[..., start:start + s])
        start += s
    return out


def _rms_norm(x, g):
    xf = x.astype(F32)
    y = xf * lax.rsqrt(jnp.mean(xf * xf, axis=-1, keepdims=True) + EPS)
    return (y * g.astype(F32)).astype(x.dtype)


def _partial_rope(x, pos):
    d = x.shape[-1]
    rot = d // 4
    half = rot // 2
    inv = ROPE_THETA ** (-jnp.arange(half, dtype=F32) * (2.0 / rot))
    ang = pos.astype(F32)[:, None] * inv[None, :]
    cos = jnp.cos(ang)[None, :, None, :]
    sin = jnp.sin(ang)[None, :, None, :]
    xf = x.astype(F32)
    x1, x2, rest = xf[..., :half], xf[..., half:rot], xf[..., rot:]
    return jnp.concatenate([x1 * cos - x2 * sin, x2 * cos + x1 * sin, rest], axis=-1).astype(x.dtype)


def _causal_dwconv(x, buf, w, b):
    width = w.shape[0]
    T = x.shape[1]
    xp = jnp.concatenate([buf.astype(x.dtype), x], axis=1)
    y = b
    for j in range(width):
        y = y + xp[:, j:j + T] * w[j]
    return y, xp[:, T:]


def _rg_lru(x, h0, w_a, b_a, w_x, b_x, lam):
    Bn, T, _ = x.shape
    xb = x.reshape(Bn, T, N_RNN_BLOCKS, RNN_BLOCK)
    r = jax.nn.sigmoid((jnp.einsum('btnk,nkj->btnj', xb, w_a).reshape(Bn, T, D_RNN) + b_a).astype(F32))
    i = jax.nn.sigmoid((jnp.einsum('btnk,nkj->btnj', xb, w_x).reshape(Bn, T, D_RNN) + b_x).astype(F32))
    log_a = -LRU_C * r * jax.nn.softplus(-lam.astype(F32))
    a = jnp.exp(log_a)
    u = jnp.sqrt(-jnp.expm1(2.0 * log_a)) * i * x.astype(F32)

    def step(h, au):
        a_t, u_t = au
        h = a_t * h + u_t
        return h, h

    h_last, hs = lax.scan(step, h0.astype(F32), (jnp.swapaxes(a, 0, 1), jnp.swapaxes(u, 0, 1)))
    return jnp.swapaxes(hs, 0, 1).astype(x.dtype), h_last


def _diff_attention(q, k, v, qpos, lam):
    Bn, T = q.shape[:2]
    L = k.shape[1]
    qb = math.gcd(T, Q_BLOCK)
    nb = T // qb
    kpos = jnp.arange(L, dtype=jnp.int32)
    scale = DK_DIFF ** -0.5
    qs = jnp.moveaxis(q.reshape(Bn, nb, qb, H_DIFF, 2, DK_DIFF), 1, 0)

    def blk(args):
        qi, pi = args
        s = jnp.einsum('bqhcd,bkhcd->bchqk', qi, k).astype(F32) * scale
        s = jnp.where(kpos[None, :] <= pi[:, None], s, -1e30)
        p = jax.nn.softmax(s, axis=-1)
        att = p[:, 0] - lam * p[:, 1]
        return jnp.einsum('bhqk,bkhd->bqhd', att.astype(v.dtype), v)

    o = lax.map(blk, (qs, qpos.reshape(nb, qb)))
    return jnp.moveaxis(o, 0, 1).reshape(Bn, T, H_DIFF, DV_DIFF)


def _dsa_attention(q, k, v, iq, ik, iw, qpos):
    Bn, T = q.shape[:2]
    L = k.shape[1]
    n_sel = min(TOPK_MAX, L // 4)
    qb = math.gcd(T, DSA_Q_BLOCK)
    nb = T // qb
    kpos = jnp.arange(L, dtype=jnp.int32)

    def blocks(a):
        return jnp.moveaxis(a.reshape(Bn, nb, qb, *a.shape[2:]), 1, 0)

    def blk(args):
        qi, iqi, iwi, pi = args
        dots = jnp.einsum('bqhd,bkd->bqhk', iqi, ik).astype(F32) * (D_IDX ** -0.5)
        score = jnp.einsum('bqh,bqhk->bqk', iwi.astype(F32), jax.nn.relu(dots))
        score = jnp.where(kpos[None, None, :] <= pi[None, :, None], score, -jnp.inf)
        _, idx = lax.top_k(score, n_sel)
        kg = jax.vmap(lambda kk, ii: kk[ii])(k, idx)
        vg = jax.vmap(lambda vv, ii: vv[ii])(v, idx)
        s = jnp.einsum('bqhd,bqkhd->bqhk', qi, kg).astype(F32) * (DH_DSA ** -0.5)
        s = jnp.where((idx <= pi[None, :, None])[:, :, None, :], s, -1e30)
        p = jax.nn.softmax(s, axis=-1)
        return jnp.einsum('bqhk,bqkhd->bqhd', p.astype(v.dtype), vg)

    o = lax.map(blk, (blocks(q), blocks(iq), blocks(iw), qpos.reshape(nb, qb)))
    return jnp.moveaxis(o, 0, 1).reshape(Bn, T, H_DSA, DH_DSA)


def _layer(x, c, pos, l, past_dk, past_dv, past_sk, past_sv, past_ik, h0, lru_buf, ffn_buf, W):
    Bn, T, _ = x.shape
    mod = jax.nn.silu(c) @ W['w_ada'][l] + W['b_ada'][l]
    sh1, sc1, ga1, sh2, sc2, ga2 = jnp.split(mod[:, None, :], 6, axis=-1)

    h = _rms_norm(x, W['g_pre_mix'][l]) * (1 + sc1) + sh1
    z = h @ W['w_in'][l]
    (x_lru, y_lru, q_d, k_d, v_d, q_s, k_s, v_s, q_i, k_i, w_i, z_gate) = _split(z, IN_SIZES)

    xc, lru_buf_new = _causal_dwconv(x_lru, lru_buf, W['lru_conv_w'][l], W['lru_conv_b'][l])
    hs, h_last = _rg_lru(xc, h0, W['lru_wa'][l], W['lru_ba'][l], W['lru_wx'][l], W['lru_bx'][l],
                         W['lru_lambda'][l])
    o_lru = hs * jax.nn.gelu(y_lru, approximate=True)

    qd = _partial_rope(q_d.reshape(Bn, T, 2 * H_DIFF, DK_DIFF), pos).reshape(Bn, T, H_DIFF, 2, DK_DIFF)
    kd_new = _partial_rope(k_d.reshape(Bn, T, 2 * H_DIFF, DK_DIFF), pos).reshape(Bn, T, H_DIFF, 2 * DK_DIFF)
    vd_new = v_d.reshape(Bn, T, H_DIFF, DV_DIFF)
    kd_all = jnp.concatenate([past_dk.astype(x.dtype), kd_new], axis=1).reshape(Bn, -1, H_DIFF, 2, DK_DIFF)
    vd_all = jnp.concatenate([past_dv.astype(x.dtype), vd_new], axis=1)
    lam_init = 0.8 - 0.6 * math.exp(-0.3 * l)
    lp = W['diff_lambda'][l].astype(F32)
    lam = jnp.exp(jnp.sum(lp[0] * lp[1])) - jnp.exp(jnp.sum(lp[2] * lp[3])) + lam_init
    od = _diff_attention(qd, kd_all, vd_all, pos, lam)
    od = (_rms_norm(od, W['diff_subln_g'][l]) * (1.0 - lam_init)).reshape(Bn, T, W_DIFF)

    qs_ = _partial_rope(q_s.reshape(Bn, T, H_DSA, DH_DSA), pos)
    ks_new = _partial_rope(k_s.reshape(Bn, T, H_DSA, DH_DSA), pos)
    vs_new = v_s.reshape(Bn, T, H_DSA, DH_DSA)
    qi_ = _partial_rope(q_i.reshape(Bn, T, H_IDX, D_IDX), pos)
    ki_new = _partial_rope(k_i.reshape(Bn, T, 1, D_IDX), pos)[:, :, 0]
    wi = w_i * (H_IDX ** -0.5)
    ks_all = jnp.concatenate([past_sk.astype(x.dtype), ks_new], axis=1)
    vs_all = jnp.concatenate([past_sv.astype(x.dtype), vs_new], axis=1)
    ki_all = jnp.concatenate([past_ik.astype(x.dtype), ki_new], axis=1)
    os_ = _dsa_attention(qs_, ks_all, vs_all, qi_, ki_all, wi, pos).reshape(Bn, T, W_DSA)

    g = jax.nn.sigmoid(z_gate).reshape(Bn, T, 3, D_MODEL)
    wb = W['w_branch'][l]
    merged = (g[:, :, 0] * (o_lru @ wb[:D_RNN])
              + g[:, :, 1] * (od @ wb[D_RNN:D_RNN + W_DIFF])
              + g[:, :, 2] * (os_ @ wb[D_RNN + W_DIFF:]))
    x = x + ga1 * _rms_norm(merged @ W['w_out'][l], W['g_post_mix'][l])

    h2 = _rms_norm(x, W['g_pre_ffn'][l]) * (1 + sc2) + sh2
    u = h2 @ W['w_up'][l]
    u, ffn_buf_new = _causal_dwconv(u, ffn_buf, W['ffn_conv_w'][l], W['ffn_conv_b'][l])
    ua, ub = jnp.split(u, 2, axis=-1)
    f = (jax.nn.gelu(ua, approximate=True) * ub) @ W['w_down'][l]
    x = x + ga2 * _rms_norm(f, W['g_post_ffn'][l])
    return x, (kd_new, vd_new, ks_new, vs_new, ki_new, h_last, lru_buf_new, ffn_buf_new)


def _run_group(x, c, pos, get_past, W):
    rows = []
    for l in range(DEPTH):
        x, new = _layer(x, c, pos, l, *get_past(l), W)
        rows.append(new)
    stacked = [jnp.stack([r[i] for r in rows]) for i in range(8)]
    return x, stacked


def _gather_pages(pool, page_table):
    g = pool[page_table]
    return g.reshape(page_table.shape[0], -1, *pool.shape[2:])


def setup_inputs(seed: int = 0) -> dict:
    key = jax.random.key(seed)
    ks = jax.random.split(key, 40)
    n_pages = PAST_LEN // PAGE_SIZE
    n_pool = (DEC_BATCH * n_pages * 5) // 4

    def nrm(i, shape, s=1.0):
        return jax.random.normal(ks[i], shape, F32) * s

    perm = jax.random.permutation(ks[0], n_pool)[:DEC_BATCH * n_pages]
    page_table = perm.reshape(DEC_BATCH, n_pages).astype(jnp.int32)
    a8 = jax.random.uniform(ks[1], (DEPTH, D_RNN), F32, minval=0.9, maxval=0.999)
    a = a8 ** (1.0 / LRU_C)
    lru_lambda = jnp.log(a) - jnp.log1p(-a)
    w_branch = jnp.concatenate([nrm(2, (DEPTH, D_RNN, D_MODEL), D_RNN ** -0.5),
                                nrm(3, (DEPTH, W_DIFF, D_MODEL), W_DIFF ** -0.5),
                                nrm(4, (DEPTH, W_DSA, D_MODEL), W_DSA ** -0.5)], axis=1)
    return {
        'x_prompt': nrm(5, (BATCH, SEQ, D_MODEL)),
        'x_sample': nrm(6, (DEC_BATCH, DEC_SEQ, D_MODEL)),
        'cache_diff_k': nrm(7, (DEPTH, n_pool, PAGE_SIZE, H_DIFF, 2 * DK_DIFF)),
        'cache_diff_v': nrm(8, (DEPTH, n_pool, PAGE_SIZE, H_DIFF, DV_DIFF)),
        'cache_dsa_k': nrm(9, (DEPTH, n_pool, PAGE_SIZE, H_DSA, DH_DSA)),
        'cache_dsa_v': nrm(10, (DEPTH, n_pool, PAGE_SIZE, H_DSA, DH_DSA)),
        'cache_idx_k': nrm(11, (DEPTH, n_pool, PAGE_SIZE, D_IDX)),
        'state_lru_h': nrm(12, (DEPTH, DEC_BATCH, D_RNN), 0.5),
        'state_lru_conv': nrm(13, (DEPTH, DEC_BATCH, CONV_W - 1, D_RNN)),
        'state_ffn_conv': nrm(14, (DEPTH, DEC_BATCH, FFN_CONV_W - 1, 2 * D_FF)),
        'page_table': page_table,
        'c_prompt': nrm(15, (BATCH, D_MODEL)),
        'c_sample': nrm(16, (DEC_BATCH, D_MODEL)),
        'w_ada': nrm(17, (DEPTH, D_MODEL, 6 * D_MODEL), 0.5 * D_MODEL ** -0.5),
        'b_ada': nrm(18, (DEPTH, 6 * D_MODEL), 0.01),
        'g_pre_mix': 1.0 + nrm(19, (DEPTH, D_MODEL), 0.05),
        'g_post_mix': 1.0 + nrm(20, (DEPTH, D_MODEL), 0.05),
        'g_pre_ffn': 1.0 + nrm(21, (DEPTH, D_MODEL), 0.05),
        'g_post_ffn': 1.0 + nrm(22, (DEPTH, D_MODEL), 0.05),
        'w_in': nrm(23, (DEPTH, D_MODEL, IN_WIDTH), D_MODEL ** -0.5),
        'lru_conv_w': nrm(24, (DEPTH, CONV_W, D_RNN), CONV_W ** -0.5),
        'lru_conv_b': nrm(25, (DEPTH, D_RNN), 0.01),
        'lru_wa': nrm(26, (DEPTH, N_RNN_BLOCKS, RNN_BLOCK, RNN_BLOCK), RNN_BLOCK ** -0.5),
        'lru_ba': nrm(27, (DEPTH, D_RNN), 0.01),
        'lru_wx': nrm(28, (DEPTH, N_RNN_BLOCKS, RNN_BLOCK, RNN_BLOCK), RNN_BLOCK ** -0.5),
        'lru_bx': nrm(29, (DEPTH, D_RNN), 0.01),
        'lru_lambda': lru_lambda,
        'diff_lambda': nrm(30, (DEPTH, 4, DK_DIFF), 0.1),
        'diff_subln_g': 1.0 + nrm(31, (DEPTH, DV_DIFF), 0.05),
        'w_branch': w_branch,
        'w_out': nrm(32, (DEPTH, D_MODEL, D_MODEL), D_MODEL ** -0.5),
        'w_up': nrm(33, (DEPTH, D_MODEL, 2 * D_FF), D_MODEL ** -0.5),
        'ffn_conv_w': nrm(34, (DEPTH, FFN_CONV_W, 2 * D_FF), FFN_CONV_W ** -0.5),
        'ffn_conv_b': nrm(35, (DEPTH, 2 * D_FF), 0.01),
        'w_down': nrm(36, (DEPTH, D_FF, D_MODEL), D_FF ** -0.5),
    }


def reference(x_prompt, x_sample, cache_diff_k, cache_diff_v, cache_dsa_k, cache_dsa_v, cache_idx_k,
              state_lru_h, state_lru_conv, state_ffn_conv, page_table, c_prompt, c_sample,
              w_ada, b_ada, g_pre_mix, g_post_mix, g_pre_ffn, g_post_ffn, w_in,
              lru_conv_w, lru_conv_b, lru_wa, lru_ba, lru_wx, lru_bx, lru_lambda,
              diff_lambda, diff_subln_g, w_branch, w_out, w_up, ffn_conv_w, ffn_conv_b, w_down):
    W = dict(w_ada=w_ada, b_ada=b_ada, g_pre_mix=g_pre_mix, g_post_mix=g_post_mix,
             g_pre_ffn=g_pre_ffn, g_post_ffn=g_post_ffn, w_in=w_in,
             lru_conv_w=lru_conv_w, lru_conv_b=lru_conv_b, lru_wa=lru_wa, lru_ba=lru_ba,
             lru_wx=lru_wx, lru_bx=lru_bx, lru_lambda=lru_lambda,
             diff_lambda=diff_lambda, diff_subln_g=diff_subln_g, w_branch=w_branch, w_out=w_out,
             w_up=w_up, ffn_conv_w=ffn_conv_w, ffn_conv_b=ffn_conv_b, w_down=w_down)

    bp, tp = x_prompt.shape[0], x_prompt.shape[1]
    dt = x_prompt.dtype
    pos_p = jnp.arange(tp, dtype=jnp.int32)

    def prompt_past(l):
        return (jnp.zeros((bp, 0, H_DIFF, 2 * DK_DIFF), dt), jnp.zeros((bp, 0, H_DIFF, DV_DIFF), dt),
                jnp.zeros((bp, 0, H_DSA, DH_DSA), dt), jnp.zeros((bp, 0, H_DSA, DH_DSA), dt),
                jnp.zeros((bp, 0, D_IDX), dt), jnp.zeros((bp, D_RNN), F32),
                jnp.zeros((bp, CONV_W - 1, D_RNN), dt), jnp.zeros((bp, FFN_CONV_W - 1, 2 * D_FF), dt))

    y_prompt, p_new = _run_group(x_prompt, c_prompt, pos_p, prompt_past, W)
    p_diff_k, p_diff_v, p_dsa_k, p_dsa_v, p_idx_k, p_lru_h, p_lru_conv, p_ffn_conv = p_new

    past_len = page_table.shape[1] * cache_diff_k.shape[2]
    pos_s = past_len + jnp.arange(x_sample.shape[1], dtype=jnp.int32)

    def sample_past(l):
        return (_gather_pages(cache_diff_k[l], page_table), _gather_pages(cache_diff_v[l], page_table),
                _gather_pages(cache_dsa_k[l], page_table), _gather_pages(cache_dsa_v[l], page_table),
                _gather_pages(cache_idx_k[l], page_table), state_lru_h[l],
                state_lru_conv[l], state_ffn_conv[l])

    y_sample, s_new = _run_group(x_sample, c_sample, pos_s, sample_past, W)
    s_diff_k, s_diff_v, s_dsa_k, s_dsa_v, s_idx_k, s_lru_h, s_lru_conv, s_ffn_conv = s_new

    return (y_prompt, y_sample,
            p_diff_k, p_diff_v, p_dsa_k, p_dsa_v, p_idx_k, p_lru_h, p_lru_conv, p_ffn_conv,
            s_diff_k, s_diff_v, s_dsa_k, s_dsa_v, s_idx_k, s_lru_h, s_lru_conv, s_ffn_conv)
```

```python
import functools
import math

import jax
import jax.numpy as jnp
from jax import lax
from jax.experimental import pallas as pl
from jax.experimental.pallas import tpu as pltpu

F32 = jnp.float32
MXU_DTYPE = jnp.bfloat16

N_RNN_BLOCKS = 16
CONV_W = 4
LRU_C = 8.0
H_DIFF, DK_DIFF = 4, 64
DV_DIFF = 2 * DK_DIFF
H_DSA, DH_DSA = 8, 64
H_IDX, D_IDX = 8, 64
TOPK_MAX = 256
ROPE_THETA = 500000.0
FFN_CONV_W = 3
EPS = 1e-6
NEG_BIG = -1e30

LANES = 128
SUBLANES = 8
MXU_TILE = 256
MIB = 1024 * 1024

ZT = 512
COL_XL, COL_YL = 0, 1024
COL_QD, COL_KD, COL_VD = 2048, 2560, 3072
COL_QS, COL_KS, COL_VS = 3584, 4096, 4608
COL_QI, COL_SM, COL_G = 5120, 5632, 6144
ZW = 9216
ROPE_FULL_TILES = tuple(c // ZT for c in (COL_QD, COL_KD, COL_QS, COL_KS, COL_QI))
ROPE_HEAD_TILE = COL_SM // ZT


def _cparams(n_axes, vmem_mib):
    return pltpu.CompilerParams(dimension_semantics=("arbitrary",) * n_axes,
                                vmem_limit_bytes=vmem_mib * MIB)


def _sigmoid(x):
    return 1.0 / (1.0 + jnp.exp(-x))


def _gelu_tanh(x):
    return 0.5 * x * (1.0 + jnp.tanh(math.sqrt(2.0 / math.pi) * (x + 0.044715 * (x * x * x))))


def _dot(a, b):
    return jnp.dot(a, b, preferred_element_type=F32)


def _dot_nt(a, b):
    return lax.dot_general(a, b, (((1,), (1,)), ((), ())), preferred_element_type=F32)


def _rms(x, g):
    return x * lax.rsqrt(jnp.mean(x * x, axis=-1, keepdims=True) + EPS) * g


def _ada_kernel(c_ref, w_ref, b_ref, o_ref):
    c = c_ref[...]
    a = (c * _sigmoid(c)).astype(MXU_DTYPE)
    o_ref[...] = _dot(a, w_ref[...].astype(MXU_DTYPE)) + b_ref[...]


def _ada(c_all, w_ada, b_ada):
    depth, d, w6 = w_ada.shape
    rows = c_all.shape[0]
    tn = 1536
    return pl.pallas_call(
        _ada_kernel,
        grid=(depth, w6 // tn),
        in_specs=[pl.BlockSpec((rows, d), lambda l, j: (0, 0)),
                  pl.BlockSpec((None, d, tn), lambda l, j: (l, 0, j)),
                  pl.BlockSpec((None, 1, tn), lambda l, j: (l, 0, j))],
        out_specs=pl.BlockSpec((None, rows, tn), lambda l, j: (l, 0, j)),
        out_shape=jax.ShapeDtypeStruct((depth, rows, w6), F32),
        compiler_params=_cparams(2, 40),
        name="ada_mod",
    )(c_all, w_ada, b_ada.reshape(depth, 1, w6))


def _in_proj_kernel(x_ref, g_ref, sc_ref, sh_ref, w_ref, rc_ref, rs1_ref, rs2_ref, z_ref, h_ref):
    j = pl.program_id(1)

    @pl.when(j == 0)
    def _():
        y = _rms(x_ref[...], g_ref[...])
        h_ref[...] = (y * (1.0 + sc_ref[0]) + sh_ref[0]).astype(h_ref.dtype)

    acc = _dot(h_ref[...], w_ref[...])

    def roped(n_chunks):
        outs = []
        for c in range(ZT // LANES):
            ch = acc[:, LANES * c:LANES * (c + 1)]
            if c < n_chunks:
                ch = (ch * rc_ref[...] + pltpu.roll(ch, LANES - 8, 1) * rs1_ref[...]
                      + pltpu.roll(ch, 8, 1) * rs2_ref[...])
            outs.append(ch)
        return jnp.concatenate(outs, axis=1)

    is_full = j == ROPE_FULL_TILES[0]
    for t in ROPE_FULL_TILES[1:]:
        is_full = is_full | (j == t)
    is_head = j == ROPE_HEAD_TILE

    @pl.when(is_full)
    def _():
        z_ref[...] = roped(ZT // LANES)

    @pl.when(is_head)
    def _():
        z_ref[...] = roped(1)

    @pl.when(jnp.logical_not(is_full | is_head))
    def _():
        z_ref[...] = acc


def _in_proj(x, g, sc, sh, w, rope, *, tm, tiles_per_mod, rope_tiles):
    n, d = x.shape
    r = sc.shape[1]
    rc, rs1, rs2 = rope
    mod_spec = pl.BlockSpec((1, r, d), lambda i, j: (i // tiles_per_mod, 0, 0))
    rope_spec = pl.BlockSpec((tm, LANES), lambda i, j: (i % rope_tiles, 0))
    return pl.pallas_call(
        _in_proj_kernel,
        grid=(n // tm, ZW // ZT),
        in_specs=[pl.BlockSpec((tm, d), lambda i, j: (i, 0)),
                  pl.BlockSpec((1, d), lambda i, j: (0, 0)),
                  mod_spec, mod_spec,
                  pl.BlockSpec((d, ZT), lambda i, j: (0, j)),
                  rope_spec, rope_spec, rope_spec],
        out_specs=pl.BlockSpec((tm, ZT), lambda i, j: (i, j)),
        out_shape=jax.ShapeDtypeStruct((n, ZW), F32),
        scratch_shapes=[pltpu.VMEM((tm, d), MXU_DTYPE)],
        compiler_params=_cparams(2, 48),
        name="in_proj",
    )(x, g, sc, sh, w, rc, rs1, rs2)


def _lru_kernel(xl_ref, yl_ref, stc_ref, sth_ref, cw_ref, cb_ref, wa_ref, ba_ref, wx_ref, bx_ref,
                lam_ref, o_ref, hlast_ref, convout_ref, xbuf, hcar, *, tt):
    ti = pl.program_id(1)
    w1 = CONV_W - 1
    c = xl_ref.shape[1]

    @pl.when(ti == 0)
    def _():
        xbuf[SUBLANES - w1:SUBLANES, :] = stc_ref[0]
        hcar[...] = sth_ref[0]

    @pl.when(ti > 0)
    def _():
        xbuf[0:SUBLANES, :] = xbuf[tt:tt + SUBLANES, :]

    x = xl_ref[...]
    xbuf[SUBLANES:SUBLANES + tt, :] = x
    xc = cb_ref[...] + cw_ref[w1:w1 + 1, :] * x
    for jj in range(w1):
        k = w1 - jj
        xc = xc + cw_ref[jj:jj + 1, :] * xbuf[SUBLANES - k:SUBLANES - k + tt, :]

    xcb = xc.astype(MXU_DTYPE)
    ra, ri = [], []
    for q in range(c // MXU_TILE):
        blk = xcb[:, MXU_TILE * q:MXU_TILE * (q + 1)]
        ra.append(_dot(blk, wa_ref[q]))
        ri.append(_dot(blk, wx_ref[q]))
    r = _sigmoid(jnp.concatenate(ra, axis=1) + ba_ref[...])
    i = _sigmoid(jnp.concatenate(ri, axis=1) + bx_ref[...])
    nl = -lam_ref[...]
    softplus = jnp.maximum(nl, 0.0) + jnp.log(1.0 + jnp.exp(-jnp.abs(nl)))
    log_a = (-LRU_C) * r * softplus
    a = jnp.exp(log_a)
    u = jnp.sqrt(1.0 - jnp.exp(2.0 * log_a)) * i * xc

    row = lax.broadcasted_iota(jnp.int32, (tt, c), 0)
    s = 1
    while s < tt:
        if s < SUBLANES:
            a_sh = jnp.where(row >= s, pltpu.roll(a, s, 0), 1.0)
            u_sh = jnp.where(row >= s, pltpu.roll(u, s, 0), 0.0)
        else:
            a_sh = jnp.concatenate([jnp.ones((s, c), F32), a[:tt - s]], axis=0)
            u_sh = jnp.concatenate([jnp.zeros((s, c), F32), u[:tt - s]], axis=0)
        u = u + a * u_sh
        a = a * a_sh
        s *= 2
    h = a * hcar[...] + u
    hcar[...] = h[tt - 1:tt]
    hlast_ref[0] = h[tt - 1:tt]
    convout_ref[0] = xbuf[SUBLANES + tt - w1:SUBLANES + tt, :]
    o_ref[...] = (h * _gelu_tanh(yl_ref[...])).astype(o_ref.dtype)


def _lru(z, st_conv, st_h, cw, cb, wa_bd, ba, wx_bd, bx, lam, *, nb, t, tt):
    c = cw.shape[1]
    nt = t // tt
    const2 = lambda b, i: (0, 0)
    const3 = lambda b, i: (0, 0, 0)
    return pl.pallas_call(
        functools.partial(_lru_kernel, tt=tt),
        grid=(nb, nt),
        in_specs=[pl.BlockSpec((tt, c), lambda b, i: (b * nt + i, COL_XL // c)),
                  pl.BlockSpec((tt, c), lambda b, i: (b * nt + i, COL_YL // c)),
                  pl.BlockSpec((1, CONV_W - 1, c), lambda b, i: (b, 0, 0)),
                  pl.BlockSpec((1, 1, c), lambda b, i: (b, 0, 0)),
                  pl.BlockSpec((CONV_W, c), const2),
                  pl.BlockSpec((1, c), const2),
                  pl.BlockSpec(wa_bd.shape, const3),
                  pl.BlockSpec((1, c), const2),
                  pl.BlockSpec(wx_bd.shape, const3),
                  pl.BlockSpec((1, c), const2),
                  pl.BlockSpec((1, c), const2)],
        out_specs=[pl.BlockSpec((tt, c), lambda b, i: (b * nt + i, 0)),
                   pl.BlockSpec((1, 1, c), lambda b, i: (b, 0, 0)),
                   pl.BlockSpec((1, CONV_W - 1, c), lambda b, i: (b, 0, 0))],
        out_shape=[jax.ShapeDtypeStruct((nb * t, c), MXU_DTYPE),
                   jax.ShapeDtypeStruct((nb, 1, c), F32),
                   jax.ShapeDtypeStruct((nb, CONV_W - 1, c), F32)],
        scratch_shapes=[pltpu.VMEM((tt + SUBLANES, c), F32), pltpu.VMEM((1, c), F32)],
        compiler_params=_cparams(2, 48),
        name="rg_lru",
    )(z, z, st_conv, st_h, cw, cb, wa_bd, ba, wx_bd, bx, lam)


def _diff_lambda(lp, lam_init):
    a = jnp.sum(lp[0:1] * lp[1:2], axis=1, keepdims=True)
    b = jnp.sum(lp[2:3] * lp[3:4], axis=1, keepdims=True)
    return jnp.exp(a) - jnp.exp(b) + lam_init


def _online_update(carry, s, v):
    m, l, acc = carry
    m_new = jnp.maximum(m, jnp.max(s, axis=1, keepdims=True))
    alpha = jnp.exp(m - m_new)
    p = jnp.exp(s - m_new)
    l = alpha * l + jnp.sum(p, axis=1, keepdims=True)
    acc = alpha * acc + _dot(p.astype(MXU_DTYPE), v)
    return m_new, l, acc


def _sort_key(score):
    bits = pltpu.bitcast(score, jnp.int32)
    return jnp.where(bits < 0, bits ^ jnp.int32(0x7FFFFFFF), bits)


def _topk_select(key, n_sel, tri):
    rows, length = key.shape
    int_min = jnp.int32(-2 ** 31)

    def count_ge(t):
        return jnp.sum(jnp.where(key >= t, 1.0, 0.0), axis=1, keepdims=True)

    ans = jnp.where(count_ge(jnp.zeros((rows, 1), jnp.int32)) >= n_sel, jnp.int32(0), int_min)

    def body(it, ans):
        t = ans | lax.shift_left(jnp.int32(1), jnp.int32(30) - it)
        return jnp.where(count_ge(t) >= n_sel, t, ans)

    thr = lax.fori_loop(0, 31, body, ans)
    gt = jnp.where(key > thr, 1.0, 0.0)
    eq = jnp.where(key == thr, 1.0, 0.0)
    need = n_sel - jnp.sum(gt, axis=1, keepdims=True)
    run = jnp.zeros((rows, 1), F32)
    out = []
    for c in range(length // LANES):
        eqc = eq[:, LANES * c:LANES * (c + 1)]
        before = _dot(eqc.astype(MXU_DTYPE), tri) + run
        out.append(gt[:, LANES * c:LANES * (c + 1)] + eqc * jnp.where(before < need, 1.0, 0.0))
        run = run + jnp.sum(eqc, axis=1, keepdims=True)
    return jnp.concatenate(out, axis=1)


def _diff_attn_kernel(q_ref, k_ref, v_ref, lp_ref, g_ref, o_ref, kb, vb, *, tq, lam_init):
    qi = pl.program_id(2)

    @pl.when(qi == 0)
    def _():
        kb[...] = k_ref[...].astype(MXU_DTYPE)
        vb[...] = v_ref[...].astype(MXU_DTYPE)

    q = q_ref[...] * (DK_DIFF ** -0.5)
    lane = lax.broadcasted_iota(jnp.int32, q.shape, 1)
    qq = jnp.concatenate([jnp.where(lane < DK_DIFF, q, 0.0), jnp.where(lane >= DK_DIFF, q, 0.0)],
                         axis=0).astype(MXU_DTYPE)

    def body(kbi, carry):
        off = pl.multiple_of(kbi * tq, tq)
        kblk = kb[pl.ds(off, tq), :]
        return _online_update(carry, _dot_nt(qq, kblk), vb[pl.ds(off, tq), :])

    init = (jnp.full((2 * tq, 1), NEG_BIG, F32), jnp.zeros((2 * tq, 1), F32),
            jnp.zeros((2 * tq, DV_DIFF), F32))
    carry = lax.fori_loop(0, qi, body, init)
    off = pl.multiple_of(qi * tq, tq)
    s = _dot_nt(qq, kb[pl.ds(off, tq), :])
    r = lax.broadcasted_iota(jnp.int32, s.shape, 0)
    r = jnp.where(r >= tq, r - tq, r)
    col = lax.broadcasted_iota(jnp.int32, s.shape, 1)
    s = jnp.where(col <= r, s, NEG_BIG)
    _, l, acc = _online_update(carry, s, vb[pl.ds(off, tq), :])

    on = acc / l
    lam = _diff_lambda(lp_ref[0], lam_init)
    o = on[:tq] - lam * on[tq:]
    o_ref[...] = (_rms(o, g_ref[...]) * (1.0 - lam_init)).astype(o_ref.dtype)


def _diff_attn_prompt(z, lp, g, *, nb, t, tq, lam_init):
    nq = t // tq
    hw = 2 * DK_DIFF
    return pl.pallas_call(
        functools.partial(_diff_attn_kernel, tq=tq, lam_init=lam_init),
        grid=(nb, H_DIFF, nq),
        in_specs=[pl.BlockSpec((tq, hw), lambda b, h, i: (b * nq + i, COL_QD // hw + h)),
                  pl.BlockSpec((t, hw), lambda b, h, i: (b, COL_KD // hw + h)),
                  pl.BlockSpec((t, DV_DIFF), lambda b, h, i: (b, COL_VD // DV_DIFF + h)),
                  pl.BlockSpec((1, 4, DK_DIFF), lambda b, h, i: (0, 0, 0)),
                  pl.BlockSpec((1, DV_DIFF), lambda b, h, i: (0, 0))],
        out_specs=pl.BlockSpec((tq, DV_DIFF), lambda b, h, i: (b * nq + i, h)),
        out_shape=jax.ShapeDtypeStruct((nb * t, H_DIFF * DV_DIFF), MXU_DTYPE),
        scratch_shapes=[pltpu.VMEM((t, hw), MXU_DTYPE), pltpu.VMEM((t, DV_DIFF), MXU_DTYPE)],
        compiler_params=_cparams(3, 40),
        name="diff_attn_prompt",
    )(z, z, z, lp, g)


def _dsa_kernel(qs_ref, ks_ref, vs_ref, qi_ref, ki_ref, wi_ref, tri_ref, o_ref, ksb, vsb, kib,
                *, tq, n_sel):
    qt = pl.program_id(1)
    length = ks_ref.shape[0]

    @pl.when(qt == 0)
    def _():
        ksb[...] = ks_ref[...].astype(MXU_DTYPE)
        vsb[...] = vs_ref[...].astype(MXU_DTYPE)
        kib[...] = ki_ref[...].astype(MXU_DTYPE)

    lane = lax.broadcasted_iota(jnp.int32, (tq, LANES), 1)
    halves = (lane < D_IDX, lane >= D_IDX)

    w = wi_ref[...] * (H_IDX ** -0.5 * D_IDX ** -0.5)
    ki = kib[...]
    score = jnp.zeros((tq, length), F32)
    for m in range(H_IDX // 2):
        ch = qi_ref[:, LANES * m:LANES * (m + 1)]
        for half in range(2):
            d = _dot_nt(jnp.where(halves[half], ch, 0.0).astype(MXU_DTYPE), ki)
            hh = 2 * m + half
            score = score + w[:, hh:hh + 1] * jnp.maximum(d, 0.0)

    col = lax.broadcasted_iota(jnp.int32, (tq, length), 1)
    qpos = qt * tq + lax.broadcasted_iota(jnp.int32, (tq, length), 0)
    valid = col <= qpos
    key = _sort_key(jnp.where(valid, score, -jnp.inf))
    sel = _topk_select(key, n_sel, tri_ref[...]) * jnp.where(valid, 1.0, 0.0)
    keep = sel > 0.5

    outs = []
    for m in range(H_DSA // 2):
        qc = qs_ref[:, LANES * m:LANES * (m + 1)] * (DH_DSA ** -0.5)
        kc = ksb[:, LANES * m:LANES * (m + 1)]
        vc = vsb[:, LANES * m:LANES * (m + 1)]
        pair = []
        for half in range(2):
            s = _dot_nt(jnp.where(halves[half], qc, 0.0).astype(MXU_DTYPE), kc)
            s = jnp.where(keep, s, NEG_BIG)
            p = jnp.exp(s - jnp.max(s, axis=1, keepdims=True))
            l = jnp.sum(p, axis=1, keepdims=True)
            pair.append(_dot(p.astype(MXU_DTYPE), vc) / l)
        outs.append(jnp.where(halves[0], pair[0], pair[1]))
    o_ref[...] = jnp.concatenate(outs, axis=1).astype(o_ref.dtype)


def _dsa_prompt(z, tri, *, nb, t, tq):
    nq = t // tq
    w = H_DSA * DH_DSA
    n_sel = min(TOPK_MAX, t // 4)
    return pl.pallas_call(
        functools.partial(_dsa_kernel, tq=tq, n_sel=n_sel),
        grid=(nb, nq),
        in_specs=[pl.BlockSpec((tq, w), lambda b, i: (b * nq + i, COL_QS // w)),
                  pl.BlockSpec((t, w), lambda b, i: (b, COL_KS // w)),
                  pl.BlockSpec((t, w), lambda b, i: (b, COL_VS // w)),
                  pl.BlockSpec((tq, w), lambda b, i: (b * nq + i, COL_QI // w)),
                  pl.BlockSpec((t, LANES), lambda b, i: (b, COL_SM // LANES)),
                  pl.BlockSpec((tq, LANES), lambda b, i: (b * nq + i, COL_SM // LANES + 1)),
                  pl.BlockSpec((LANES, LANES), lambda b, i: (0, 0))],
        out_specs=pl.BlockSpec((tq, w), lambda b, i: (b * nq + i, 0)),
        out_shape=jax.ShapeDtypeStruct((nb * t, w), MXU_DTYPE),
        scratch_shapes=[pltpu.VMEM((t, w), MXU_DTYPE), pltpu.VMEM((t, w), MXU_DTYPE),
                        pltpu.VMEM((t, LANES), MXU_DTYPE)],
        compiler_params=_cparams(2, 56),
        name="dsa_prompt",
    )(z, z, z, z, z, z, tri)


def _sample_select_kernel(pt_ref, qi_ref, wi_ref, kn_ref, tri_ref, *rest, pp, n_pages, n_sel, t):
    pages = rest[:pp]
    sel_ref, score, qrows, wcol = rest[pp:]
    p = pl.program_id(1)
    rows = H_IDX * t

    @pl.when(p == 0)
    def _():
        q = qi_ref[...]
        qrows[...] = jnp.concatenate([q[:, D_IDX * h:D_IDX * (h + 1)] for h in range(H_IDX)],
                                     axis=0).astype(MXU_DTYPE)
        w = wi_ref[...] * (H_IDX ** -0.5 * D_IDX ** -0.5)
        wcol[...] = jnp.concatenate([w[:, h:h + 1] for h in range(H_IDX)], axis=0)

    def page_scores(kpage):
        d = _dot_nt(qrows[...], kpage.astype(MXU_DTYPE))
        d = wcol[...] * jnp.maximum(d, 0.0)
        tot = d[0:t]
        for h in range(1, H_IDX):
            tot = tot + d[h * t:(h + 1) * t]
        return tot

    for r in range(pp):
        off = pl.multiple_of((p * pp + r) * LANES, LANES)
        score[:, pl.ds(off, LANES)] = page_scores(pages[r][...])

    @pl.when(p == n_pages // pp - 1)
    def _():
        new = page_scores(kn_ref[...])
        ci = lax.broadcasted_iota(jnp.int32, (t, LANES), 1)
        ri = lax.broadcasted_iota(jnp.int32, (t, LANES), 0)
        score[:, n_pages * LANES:] = jnp.where(ci <= ri, new, -jnp.inf)
        sel_ref[...] = _topk_select(_sort_key(score[...]), n_sel, tri_ref[...])


def _sample_select(z, cache_idx, page_table, tri, kn_pad, *, layer, nb, t, pp):
    n_pages = page_table.shape[1]
    page = cache_idx.shape[2]
    lpad = n_pages * page + LANES
    n_sel = min(TOPK_MAX, (n_pages * page + t) // 4)
    w = H_IDX * D_IDX
    page_specs = [pl.BlockSpec((None, None, page, D_IDX),
                               functools.partial(lambda b, p, pt, r: (layer, pt[b, p * pp + r], 0, 0), r=r))
                  for r in range(pp)]
    grid_spec = pltpu.PrefetchScalarGridSpec(
        num_scalar_prefetch=1,
        grid=(nb, n_pages // pp),
        in_specs=[pl.BlockSpec((t, w), lambda b, p, pt: (b, COL_QI // w)),
                  pl.BlockSpec((t, LANES), lambda b, p, pt: (b, COL_SM // LANES + 1)),
                  pl.BlockSpec((None, LANES, D_IDX), lambda b, p, pt: (b, 0, 0)),
                  pl.BlockSpec((LANES, LANES), lambda b, p, pt: (0, 0))] + page_specs,
        out_specs=pl.BlockSpec((None, t, lpad), lambda b, p, pt: (b, 0, 0)),
        scratch_shapes=[pltpu.VMEM((t, lpad), F32), pltpu.VMEM((H_IDX * t, D_IDX), MXU_DTYPE),
                        pltpu.VMEM((H_IDX * t, 1), F32)])
    return pl.pallas_call(
        functools.partial(_sample_select_kernel, pp=pp, n_pages=n_pages, n_sel=n_sel, t=t),
        grid_spec=grid_spec,
        out_shape=jax.ShapeDtypeStruct((nb, t, lpad), F32),
        compiler_params=_cparams(2, 32),
        name="sample_select",
    )(page_table, z, z, kn_pad, tri, *([cache_idx] * pp))


def _paged_attn_kernel(pt_ref, q_ref, kn_ref, vn_ref, *rest, pp, n_pages, t, mode, lam_init):
    if mode == "dsa":
        sel_ref, rest = rest[0], rest[1:]
    else:
        lp_ref, g_ref, rest = rest[0], rest[1], rest[2:]
    kpages, vpages = rest[:pp], rest[pp:2 * pp]
    o_ref, qbd, m_s, l_s, acc_s = rest[2 * pp:]
    p = pl.program_id(1)
    rows, width = qbd.shape
    page = kpages[0].shape[0]

    @pl.when(p == 0)
    def _():
        scale = (DH_DSA if mode == "dsa" else DK_DIFF) ** -0.5
        q = jnp.concatenate([q_ref[...] * scale] * (rows // t), axis=0)
        grp = lax.broadcasted_iota(jnp.int32, (rows, width), 0) // t
        lane_grp = lax.broadcasted_iota(jnp.int32, (rows, width), 1) // 64
        if mode == "dsa":
            want = grp
        else:
            want = 2 * (grp % H_DIFF) + grp // H_DIFF
        qbd[...] = jnp.where(lane_grp == want, q, 0.0).astype(MXU_DTYPE)
        m_s[...] = jnp.full(m_s.shape, NEG_BIG, F32)
        l_s[...] = jnp.zeros(l_s.shape, F32)
        acc_s[...] = jnp.zeros(acc_s.shape, F32)

    def update(s, v):
        m, l, acc = _online_update((m_s[...], l_s[...], acc_s[...]), s, v)
        m_s[...] = m
        l_s[...] = l
        acc_s[...] = acc

    ks = jnp.concatenate([r[...] for r in kpages], axis=0).astype(MXU_DTYPE)
    vs = jnp.concatenate([r[...] for r in vpages], axis=0).astype(MXU_DTYPE)
    s = _dot_nt(qbd[...], ks)
    if mode == "dsa":
        off = pl.multiple_of(p * (pp * page), pp * page)
        selc = sel_ref[:, pl.ds(off, pp * page)]
        s = jnp.where(jnp.concatenate([selc] * (rows // t), axis=0) > 0.5, s, NEG_BIG)
    update(s, vs)

    @pl.when(p == n_pages // pp - 1)
    def _():
        s = _dot_nt(qbd[...], kn_ref[...].astype(MXU_DTYPE))
        ci = lax.broadcasted_iota(jnp.int32, s.shape, 1)
        ri = lax.broadcasted_iota(jnp.int32, s.shape, 0) % t
        keep = ci <= ri
        if mode == "dsa":
            seln = sel_ref[:, n_pages * page:]
            keep = keep & (jnp.concatenate([seln] * (rows // t), axis=0) > 0.5)
        update(jnp.where(keep, s, NEG_BIG), vn_ref[...].astype(MXU_DTYPE))
        on = acc_s[...] / l_s[...]
        lane = lax.broadcasted_iota(jnp.int32, (t, width), 1)
        if mode == "dsa":
            out = on[0:t]
            for h in range(1, H_DSA):
                out = jnp.where(lane // DH_DSA == h, on[h * t:(h + 1) * t], out)
            o_ref[...] = out.astype(o_ref.dtype)
        else:
            comp = []
            for c in range(2):
                base = c * H_DIFF * t
                oc = on[base:base + t]
                for h in range(1, H_DIFF):
                    oc = jnp.where(lane // DV_DIFF == h, on[base + h * t:base + (h + 1) * t], oc)
                comp.append(oc)
            o = comp[0] - _diff_lambda(lp_ref[0], lam_init) * comp[1]
            heads = [_rms(o[:, DV_DIFF * h:DV_DIFF * (h + 1)], g_ref[...]) for h in range(H_DIFF)]
            o_ref[...] = (jnp.concatenate(heads, axis=1) * (1.0 - lam_init)).astype(o_ref.dtype)


def _paged_attn(z, cache_k, cache_v, page_table, kn_pad, vn_pad, extra, *, layer, nb, t, pp, mode,
                q_col, lam_init=0.0):
    n_pages = page_table.shape[1]
    page, width = cache_k.shape[2], cache_k.shape[3]
    rows = (H_DSA if mode == "dsa" else 2 * H_DIFF) * t

    def page_spec(r):
        return pl.BlockSpec((None, None, page, width),
                            functools.partial(lambda b, p, pt, r: (layer, pt[b, p * pp + r], 0, 0), r=r))

    in_specs = [pl.BlockSpec((t, width), lambda b, p, pt: (b, q_col // width)),
                pl.BlockSpec((None, LANES, width), lambda b, p, pt: (b, 0, 0)),
                pl.BlockSpec((None, LANES, width), lambda b, p, pt: (b, 0, 0))]
    if mode == "dsa":
        (sel,) = extra
        in_specs.append(pl.BlockSpec((None, t, sel.shape[2]), lambda b, p, pt: (b, 0, 0)))
    else:
        in_specs += [pl.BlockSpec((1, 4, DK_DIFF), lambda b, p, pt: (0, 0, 0)),
                     pl.BlockSpec((1, DV_DIFF), lambda b, p, pt: (0, 0))]
    in_specs += [page_spec(r) for r in range(pp)] * 2
    grid_spec = pltpu.PrefetchScalarGridSpec(
        num_scalar_prefetch=1,
        grid=(nb, n_pages // pp),
        in_specs=in_specs,
        out_specs=pl.BlockSpec((t, width), lambda b, p, pt: (b, 0)),
        scratch_shapes=[pltpu.VMEM((rows, width), MXU_DTYPE), pltpu.VMEM((rows, 1), F32),
                        pltpu.VMEM((rows, 1), F32), pltpu.VMEM((rows, width), F32)])
    return pl.pallas_call(
        functools.partial(_paged_attn_kernel, pp=pp, n_pages=n_pages, t=t, mode=mode, lam_init=lam_init),
        grid_spec=grid_spec,
        out_shape=jax.ShapeDtypeStruct((nb * t, width), MXU_DTYPE),
        compiler_params=_cparams(2, 48),
        name="paged_attn_" + mode,
    )(page_table, z, kn_pad, vn_pad, *extra, *([cache_k] * pp), *([cache_v] * pp))


def _merge_kernel(ol_ref, od_ref, os_ref, zg_ref, x_ref, ga_ref, wl_ref, wd_ref, ws_ref, wo_ref,
                  g_ref, o_ref):
    d = x_ref.shape[1]
    zg = zg_ref[...]
    merged = (_sigmoid(zg[:, 0:d]) * _dot(ol_ref[...], wl_ref[...])
              + _sigmoid(zg[:, d:2 * d]) * _dot(od_ref[...], wd_ref[...])
              + _sigmoid(zg[:, 2 * d:3 * d]) * _dot(os_ref[...], ws_ref[...]))
    y = _dot(merged.astype(MXU_DTYPE), wo_ref[...])
    o_ref[...] = x_ref[...] + ga_ref[0] * _rms(y, g_ref[...])


def _merge(o_lru, od, os_, z, x, ga, wl, wd, ws, wo, g, *, tm, tiles_per_mod):
    n, d = x.shape
    r = ga.shape[1]
    row = lambda i: (i, 0)
    const = lambda i: (0, 0)
    return pl.pallas_call(
        _merge_kernel,
        grid=(n // tm,),
        in_specs=[pl.BlockSpec((tm, o_lru.shape[1]), row),
                  pl.BlockSpec((tm, od.shape[1]), row),
                  pl.BlockSpec((tm, os_.shape[1]), row),
                  pl.BlockSpec((tm, 3 * d), lambda i: (i, COL_G // (3 * d))),
                  pl.BlockSpec((tm, d), row),
                  pl.BlockSpec((1, r, d), lambda i: (i // tiles_per_mod, 0, 0)),
                  pl.BlockSpec(wl.shape, const), pl.BlockSpec(wd.shape, const),
                  pl.BlockSpec(ws.shape, const), pl.BlockSpec(wo.shape, const),
                  pl.BlockSpec((1, d), const)],
        out_specs=pl.BlockSpec((tm, d), row),
        out_shape=jax.ShapeDtypeStruct((n, d), F32),
        compiler_params=_cparams(1, 48),
        name="merge_out",
    )(o_lru, od, os_, z, x, ga, wl, wd, ws, wo, g)


HALO = 16


def _ffn_up_kernel(x_ref, xh_ref, g_ref, sc_ref, sh_ref, wa_ref, wb_ref, cwa_ref, cwb_ref, cba_ref,
                   cbb_ref, sta_ref, stb_ref, f_ref, keepa_ref, keepb_ref, h_s, ua_s, ub_s,
                   *, tm, seq, keep):
    i = pl.program_id(0)
    j = pl.program_id(1)
    w1 = FFN_CONV_W - 1
    long_seq = seq >= tm

    @pl.when(j == 0)
    def _():
        def normed(xv):
            return (_rms(xv, g_ref[...]) * (1.0 + sc_ref[0]) + sh_ref[0]).astype(h_s.dtype)
        if long_seq:
            h_s[0:HALO, :] = normed(xh_ref[...])
        else:
            h_s[0:HALO, :] = jnp.zeros((HALO, h_s.shape[1]), h_s.dtype)
        h_s[HALO:, :] = normed(x_ref[...])

    def half(w_ref, cw_ref, cb_ref, st_ref, keep_ref, u_s):
        u_s[...] = _dot(h_s[...], w_ref[...])
        if long_seq:
            @pl.when(i % (seq // tm) == 0)
            def _():
                u_s[HALO - w1:HALO, :] = st_ref[0]
        u = u_s[HALO:, :]
        keep_ref[...] = u_s[HALO + tm - keep:, :]
        y = cb_ref[...] + cw_ref[w1:w1 + 1, :] * u
        if not long_seq:
            tpos = lax.broadcasted_iota(jnp.int32, u.shape, 0) % seq
        for jj in range(w1):
            k = w1 - jj
            prev = u_s[HALO - k:HALO - k + tm, :]
            if not long_seq:
                prev = jnp.where(tpos >= k, prev, st_ref[jj])
            y = y + cw_ref[jj:jj + 1, :] * prev
        return y

    ya = half(wa_ref, cwa_ref, cba_ref, sta_ref, keepa_ref, ua_s)
    yb = half(wb_ref, cwb_ref, cbb_ref, stb_ref, keepb_ref, ub_s)
    f_ref[...] = (_gelu_tanh(ya) * yb).astype(f_ref.dtype)


def _ffn_up(x, g, sc, sh, w_up, cw, cb, st, *, tm, tn, seq, tiles_per_mod):
    n, d = x.shape
    dff = w_up.shape[1] // 2
    nj = dff // tn
    r = sc.shape[1]
    long_seq = seq >= tm
    keep = SUBLANES if long_seq else tm
    mod_spec = pl.BlockSpec((1, r, d), lambda i, j: (i // tiles_per_mod, 0, 0))
    if long_seq:
        tps = seq // tm
        sta = pl.BlockSpec((1, FFN_CONV_W - 1, tn), lambda i, j: (i // tps, 0, j))
        stb = pl.BlockSpec((1, FFN_CONV_W - 1, tn), lambda i, j: (i // tps, 0, nj + j))
    else:
        sta = pl.BlockSpec((FFN_CONV_W - 1, tm, tn), lambda i, j: (0, i, j))
        stb = pl.BlockSpec((FFN_CONV_W - 1, tm, tn), lambda i, j: (0, i, nj + j))
    n_keep = (n // tm) * keep
    return pl.pallas_call(
        functools.partial(_ffn_up_kernel, tm=tm, seq=seq, keep=keep),
        grid=(n // tm, nj),
        in_specs=[pl.BlockSpec((tm, d), lambda i, j: (i, 0)),
                  pl.BlockSpec((HALO, d), lambda i, j: (jnp.maximum(i * (tm // HALO) - 1, 0), 0)),
                  pl.BlockSpec((1, d), lambda i, j: (0, 0)),
                  mod_spec, mod_spec,
                  pl.BlockSpec((d, tn), lambda i, j: (0, j)),
                  pl.BlockSpec((d, tn), lambda i, j: (0, nj + j)),
                  pl.BlockSpec((FFN_CONV_W, tn), lambda i, j: (0, j)),
                  pl.BlockSpec((FFN_CONV_W, tn), lambda i, j: (0, nj + j)),
                  pl.BlockSpec((1, tn), lambda i, j: (0, j)),
                  pl.BlockSpec((1, tn), lambda i, j: (0, nj + j)),
                  sta, stb],
        out_specs=[pl.BlockSpec((tm, tn), lambda i, j: (i, j)),
                   pl.BlockSpec((keep, tn), lambda i, j: (i, j)),
                   pl.BlockSpec((keep, tn), lambda i, j: (i, j))],
        out_shape=[jax.ShapeDtypeStruct((n, dff), MXU_DTYPE),
                   jax.ShapeDtypeStruct((n_keep, dff), F32),
                   jax.ShapeDtypeStruct((n_keep, dff), F32)],
        scratch_shapes=[pltpu.VMEM((HALO + tm, d), MXU_DTYPE), pltpu.VMEM((HALO + tm, tn), F32),
                        pltpu.VMEM((HALO + tm, tn), F32)],
        compiler_params=_cparams(2, 48),
        name="ffn_up",
    )(x, x, g, sc, sh, w_up, w_up, cw, cw, cb, cb, st, st)


def _ffn_down_kernel(f_ref, w_ref, x_ref, ga_ref, g_ref, o_ref):
    y = _dot(f_ref[...], w_ref[...])
    o_ref[...] = x_ref[...] + ga_ref[0] * _rms(y, g_ref[...])


def _ffn_down(f, w, x, ga, g, *, tm, tiles_per_mod):
    n, d = x.shape
    r = ga.shape[1]
    return pl.pallas_call(
        _ffn_down_kernel,
        grid=(n // tm,),
        in_specs=[pl.BlockSpec((tm, f.shape[1]), lambda i: (i, 0)),
                  pl.BlockSpec(w.shape, lambda i: (0, 0)),
                  pl.BlockSpec((tm, d), lambda i: (i, 0)),
                  pl.BlockSpec((1, r, d), lambda i: (i // tiles_per_mod, 0, 0)),
                  pl.BlockSpec((1, d), lambda i: (0, 0))],
        out_specs=pl.BlockSpec((tm, d), lambda i: (i, 0)),
        out_shape=jax.ShapeDtypeStruct((n, d), F32),
        compiler_params=_cparams(1, 48),
        name="ffn_down",
    )(f, w, x, ga, g)


def _rope_tables(pos):
    rot = D_IDX // 4
    half = rot // 2
    inv = ROPE_THETA ** (-jnp.arange(half, dtype=F32) * (2.0 / rot))
    ang = pos.astype(F32)[:, None] * inv[None, :]
    cos, sin = jnp.cos(ang), jnp.sin(ang)
    p = pos.shape[0]
    one = jnp.ones((p, D_IDX - rot), F32)
    zero = jnp.zeros((p, D_IDX - rot), F32)
    zh = jnp.zeros((p, half), F32)
    c = jnp.concatenate([cos, cos, one], axis=1)
    s1 = jnp.concatenate([-sin, zh, zero], axis=1)
    s2 = jnp.concatenate([zh, sin, zero], axis=1)
    return tuple(jnp.tile(a, (1, LANES // D_IDX)) for a in (c, s1, s2))


def _pack_w_in(w_in):
    depth, d, _ = w_in.shape
    n_main = COL_SM
    ki = w_in[:, :, n_main:n_main + D_IDX]
    wi = w_in[:, :, n_main + D_IDX:n_main + D_IDX + H_IDX]
    gate = w_in[:, :, n_main + D_IDX + H_IDX:]
    pad = jnp.zeros((depth, d, COL_G - COL_SM - 2 * D_IDX - H_IDX), w_in.dtype)
    return jnp.concatenate([w_in[:, :, :n_main], ki, ki, wi, pad, gate], axis=2).astype(MXU_DTYPE)


def _block_diag(w, per):
    depth, nblk, k, _ = w.shape
    w = w.reshape(depth, nblk // per, per, k, k)
    eye = jnp.eye(per, dtype=w.dtype)
    return jnp.einsum("dgpij,pq->dgpiqj", w, eye).reshape(depth, nblk // per, per * k, per * k)


def _pad_rows(a, nb, t):
    a = a.reshape(nb, t, a.shape[1])
    return jnp.pad(a, ((0, 0), (0, LANES - t), (0, 0)))


def kernel(x_prompt, x_sample, cache_diff_k, cache_diff_v, cache_dsa_k, cache_dsa_v, cache_idx_k, state_lru_h, state_lru_conv, state_ffn_conv, page_table, c_prompt, c_sample, w_ada, b_ada, g_pre_mix, g_post_mix, g_pre_ffn, g_post_ffn, w_in, lru_conv_w, lru_conv_b, lru_wa, lru_ba, lru_wx, lru_bx, lru_lambda, diff_lambda, diff_subln_g, w_branch, w_out, w_up, ffn_conv_w, ffn_conv_b, w_down):
    bp, tp, d = x_prompt.shape
    bs, ts, _ = x_sample.shape
    depth = w_in.shape[0]
    d_rnn = lru_conv_w.shape[2]
    dff2 = w_up.shape[2]
    n_pool, page = cache_diff_k.shape[1], cache_diff_k.shape[2]
    past_len = page_table.shape[1] * page
    w_diff, w_dsa = H_DIFF * DV_DIFF, H_DSA * DH_DSA
    np_, ns_ = bp * tp, bs * ts

    w_in_p = _pack_w_in(w_in)
    wa_bd = _block_diag(lru_wa, MXU_TILE // (d_rnn // N_RNN_BLOCKS)).astype(MXU_DTYPE)
    wx_bd = _block_diag(lru_wx, MXU_TILE // (d_rnn // N_RNN_BLOCKS)).astype(MXU_DTYPE)
    wb = w_branch.astype(MXU_DTYPE)
    wo = w_out.astype(MXU_DTYPE)
    wu = w_up.astype(MXU_DTYPE)
    wd = w_down.astype(MXU_DTYPE)
    tri = jnp.triu(jnp.ones((LANES, LANES), F32), k=1).astype(MXU_DTYPE)
    row2 = lambda a: a[:, None, :]

    mod = _ada(jnp.concatenate([c_prompt, c_sample], axis=0), w_ada, b_ada)

    rope_p = _rope_tables(jnp.arange(tp, dtype=jnp.int32))
    pos_s = past_len + jnp.arange(ts, dtype=jnp.int32)
    rope_s = _rope_tables(jnp.tile(pos_s, bs))

    cdk = cache_diff_k.reshape(depth, n_pool, page, w_diff)
    cdv = cache_diff_v.reshape(depth, n_pool, page, w_diff)
    csk = cache_dsa_k.reshape(depth, n_pool, page, w_dsa)
    csv = cache_dsa_v.reshape(depth, n_pool, page, w_dsa)

    tm_p, tq_p, tt_p = 512, 256, 256
    xp = x_prompt.reshape(np_, d)
    xs = x_sample.reshape(ns_, d)
    outs_p, outs_s = [], []
    zeros_conv = jnp.zeros((bp, CONV_W - 1, d_rnn), F32)
    zeros_h = jnp.zeros((bp, 1, d_rnn), F32)
    zeros_ffn = jnp.zeros((bp, FFN_CONV_W - 1, dff2), F32)

    for l in range(depth):
        lam_init = 0.8 - 0.6 * math.exp(-0.3 * l)
        m = mod[l]
        chunks = [m[:, k * d:(k + 1) * d] for k in range(6)]
        mp = [c[:bp][:, None, :] for c in chunks]
        ms = [jnp.repeat(c[bp:], ts, axis=0)[None] for c in chunks]
        lru_w = (lru_conv_w[l], row2(lru_conv_b)[l], wa_bd[l], row2(lru_ba)[l], wx_bd[l],
                 row2(lru_bx)[l], row2(lru_lambda)[l])
        lp = diff_lambda[l][None]
        gsub = row2(diff_subln_g)[l]
        wl_, wd_, ws_ = wb[l, :d_rnn], wb[l, d_rnn:d_rnn + w_diff], wb[l, d_rnn + w_diff:]

        z = _in_proj(xp, row2(g_pre_mix)[l], mp[1], mp[0], w_in_p[l], rope_p,
                     tm=tm_p, tiles_per_mod=tp // tm_p, rope_tiles=tp // tm_p)
        o_lru, h_last, conv_new = _lru(z, zeros_conv, zeros_h, *lru_w, nb=bp, t=tp, tt=tt_p)
        od = _diff_attn_prompt(z, lp, gsub, nb=bp, t=tp, tq=tq_p, lam_init=lam_init)
        os_ = _dsa_prompt(z, tri, nb=bp, t=tp, tq=tq_p)
        x1 = _merge(o_lru, od, os_, z, xp, mp[2], wl_, wd_, ws_, wo[l], row2(g_post_mix)[l],
                    tm=256, tiles_per_mod=tp // 256)
        f, keep_a, keep_b = _ffn_up(x1, row2(g_pre_ffn)[l], mp[4], mp[3], wu[l], ffn_conv_w[l],
                                    row2(ffn_conv_b)[l], zeros_ffn, tm=tm_p, tn=256, seq=tp,
                                    tiles_per_mod=tp // tm_p)
        xp = _ffn_down(f, wd[l], x1, mp[5], row2(g_post_ffn)[l], tm=256, tiles_per_mod=tp // 256)
        keep_u = jnp.concatenate([keep_a, keep_b], axis=1).reshape(bp, tp // tm_p, SUBLANES, dff2)
        outs_p.append((z[:, COL_KD:COL_KD + w_diff].reshape(bp, tp, H_DIFF, 2 * DK_DIFF),
                       z[:, COL_VD:COL_VD + w_diff].reshape(bp, tp, H_DIFF, DV_DIFF),
                       z[:, COL_KS:COL_KS + w_dsa].reshape(bp, tp, H_DSA, DH_DSA),
                       z[:, COL_VS:COL_VS + w_dsa].reshape(bp, tp, H_DSA, DH_DSA),
                       z[:, COL_SM:COL_SM + D_IDX].reshape(bp, tp, D_IDX),
                       h_last[:, 0], conv_new,
                       keep_u[:, -1, SUBLANES - (FFN_CONV_W - 1):]))

        z = _in_proj(xs, row2(g_pre_mix)[l], ms[1], ms[0], w_in_p[l], rope_s,
                     tm=ns_, tiles_per_mod=1, rope_tiles=1)
        o_lru, h_last, conv_new = _lru(z, state_lru_conv[l], state_lru_h[l][:, None, :], *lru_w,
                                       nb=bs, t=ts, tt=ts)
        kn_i = _pad_rows(z[:, COL_SM:COL_SM + D_IDX], bs, ts)
        sel = _sample_select(z, cache_idx_k, page_table, tri, kn_i, layer=l, nb=bs, t=ts, pp=8)
        os_ = _paged_attn(z, csk, csv, page_table, _pad_rows(z[:, COL_KS:COL_KS + w_dsa], bs, ts),
                          _pad_rows(z[:, COL_VS:COL_VS + w_dsa], bs, ts), (sel,), layer=l, nb=bs,
                          t=ts, pp=8, mode="dsa", q_col=COL_QS)
        od = _paged_attn(z, cdk, cdv, page_table, _pad_rows(z[:, COL_KD:COL_KD + w_diff], bs, ts),
                         _pad_rows(z[:, COL_VD:COL_VD + w_diff], bs, ts), (lp, gsub), layer=l,
                         nb=bs, t=ts, pp=8, mode="diff", q_col=COL_QD, lam_init=lam_init)
        x1 = _merge(o_lru, od, os_, z, xs, ms[2], wl_, wd_, ws_, wo[l], row2(g_post_mix)[l],
                    tm=ns_, tiles_per_mod=1)
        stf = state_ffn_conv[l]
        st_rows = jnp.stack([
            jnp.pad(stf[:, jj:, :], ((0, 0), (0, ts - (FFN_CONV_W - 1 - jj)), (0, 0))).reshape(ns_, dff2)
            for jj in range(FFN_CONV_W - 1)])
        f, keep_a, keep_b = _ffn_up(x1, row2(g_pre_ffn)[l], ms[4], ms[3], wu[l], ffn_conv_w[l],
                                    row2(ffn_conv_b)[l], st_rows, tm=ns_, tn=256, seq=ts,
                                    tiles_per_mod=1)
        xs = _ffn_down(f, wd[l], x1, ms[5], row2(g_post_ffn)[l], tm=ns_, tiles_per_mod=1)
        u_all = jnp.concatenate([keep_a, keep_b], axis=1).reshape(bs, ts, dff2)
        outs_s.append((z[:, COL_KD:COL_KD + w_diff].reshape(bs, ts, H_DIFF, 2 * DK_DIFF),
                       z[:, COL_VD:COL_VD + w_diff].reshape(bs, ts, H_DIFF, DV_DIFF),
                       z[:, COL_KS:COL_KS + w_dsa].reshape(bs, ts, H_DSA, DH_DSA),
                       z[:, COL_VS:COL_VS + w_dsa].reshape(bs, ts, H_DSA, DH_DSA),
                       z[:, COL_SM:COL_SM + D_IDX].reshape(bs, ts, D_IDX),
                       h_last[:, 0], conv_new,
                       u_all[:, ts - (FFN_CONV_W - 1):]))

    stack = lambda rows: [jnp.stack([r[k] for r in rows]) for k in range(8)]
    return (xp.reshape(bp, tp, d), xs.reshape(bs, ts, d), *stack(outs_p), *stack(outs_s))
```

```python
import functools
import math

import jax
import jax.numpy as jnp
from jax import lax
from jax.experimental import pallas as pl
from jax.experimental.pallas import tpu as pltpu

F32 = jnp.float32
MXU_DTYPE = jnp.bfloat16

N_RNN_BLOCKS = 16
CONV_W = 4
LRU_C = 8.0
H_DIFF, DK_DIFF = 4, 64
DV_DIFF = 2 * DK_DIFF
H_DSA, DH_DSA = 8, 64
H_IDX, D_IDX = 8, 64
TOPK_MAX = 256
ROPE_THETA = 500000.0
FFN_CONV_W = 3
EPS = 1e-6
NEG_BIG = -1e30

LANES = 128
SUBLANES = 8
MXU_TILE = 256
MIB = 1024 * 1024
CAUSAL_SPAN = 512

ZT = 512
COL_XL, COL_YL = 0, 1024
COL_QD, COL_KD, COL_VD = 2048, 2560, 3072
COL_QS, COL_KS, COL_VS = 3584, 4096, 4608
COL_QI, COL_SM, COL_G = 5120, 5632, 6144
ZW = 9216
ROPE_FULL_TILES = tuple(c // ZT for c in (COL_QD, COL_KD, COL_QS, COL_KS, COL_QI))
ROPE_HEAD_TILE = COL_SM // ZT


def _cparams(n_axes, vmem_mib):
    return pltpu.CompilerParams(dimension_semantics=("arbitrary",) * n_axes,
                                vmem_limit_bytes=vmem_mib * MIB)


def _sigmoid(x):
    return 1.0 / (1.0 + jnp.exp(-x))


def _gelu_tanh(x):
    return 0.5 * x * (1.0 + jnp.tanh(math.sqrt(2.0 / math.pi) * (x + 0.044715 * (x * x * x))))


def _dot(a, b):
    return jnp.dot(a, b, preferred_element_type=F32)


def _dot_nt(a, b):
    return lax.dot_general(a, b, (((1,), (1,)), ((), ())), preferred_element_type=F32)


def _rms(x, g):
    return x * lax.rsqrt(jnp.mean(x * x, axis=-1, keepdims=True) + EPS) * g


def _ada_kernel(c_ref, w_ref, b_ref, o_ref):
    c = c_ref[...]
    a = (c * _sigmoid(c)).astype(MXU_DTYPE)
    o_ref[...] = _dot(a, w_ref[...].astype(MXU_DTYPE)) + b_ref[...]


def _ada(c_all, w_ada, b_ada):
    depth, d, w6 = w_ada.shape
    rows = c_all.shape[0]
    tn = 1536
    return pl.pallas_call(
        _ada_kernel,
        grid=(depth, w6 // tn),
        in_specs=[pl.BlockSpec((rows, d), lambda l, j: (0, 0)),
                  pl.BlockSpec((None, d, tn), lambda l, j: (l, 0, j)),
                  pl.BlockSpec((None, 1, tn), lambda l, j: (l, 0, j))],
        out_specs=pl.BlockSpec((None, rows, tn), lambda l, j: (l, 0, j)),
        out_shape=jax.ShapeDtypeStruct((depth, rows, w6), F32),
        compiler_params=_cparams(2, 40),
        name="ada_mod",
    )(c_all, w_ada, b_ada.reshape(depth, 1, w6))


def _in_proj_kernel(x_ref, g_ref, sc_ref, sh_ref, w_ref, rc_ref, rs1_ref, rs2_ref, z_ref, h_ref):
    j = pl.program_id(1)

    @pl.when(j == 0)
    def _():
        y = _rms(x_ref[...], g_ref[...])
        h_ref[...] = (y * (1.0 + sc_ref[0]) + sh_ref[0]).astype(h_ref.dtype)

    acc = _dot(h_ref[...], w_ref[:, pl.ds(pl.multiple_of(j * ZT, ZT), ZT)])

    def roped(n_chunks):
        outs = []
        for c in range(ZT // LANES):
            ch = acc[:, LANES * c:LANES * (c + 1)]
            if c < n_chunks:
                ch = (ch * rc_ref[...] + pltpu.roll(ch, LANES - 8, 1) * rs1_ref[...]
                      + pltpu.roll(ch, 8, 1) * rs2_ref[...])
            outs.append(ch)
        return jnp.concatenate(outs, axis=1)

    is_full = j == ROPE_FULL_TILES[0]
    for t in ROPE_FULL_TILES[1:]:
        is_full = is_full | (j == t)
    is_head = j == ROPE_HEAD_TILE

    @pl.when(is_full)
    def _():
        z_ref[...] = roped(ZT // LANES)

    @pl.when(is_head)
    def _():
        z_ref[...] = roped(1)

    @pl.when(jnp.logical_not(is_full | is_head))
    def _():
        z_ref[...] = acc


def _in_proj(x, g, sc, sh, w, rope, *, tm, tiles_per_mod, rope_tiles):
    n, d = x.shape
    r = sc.shape[1]
    rc, rs1, rs2 = rope
    mod_spec = pl.BlockSpec((1, r, d), lambda i, j: (i // tiles_per_mod, 0, 0))
    rope_spec = pl.BlockSpec((tm, LANES), lambda i, j: (i % rope_tiles, 0))
    return pl.pallas_call(
        _in_proj_kernel,
        grid=(n // tm, ZW // ZT),
        in_specs=[pl.BlockSpec((tm, d), lambda i, j: (i, 0)),
                  pl.BlockSpec((1, d), lambda i, j: (0, 0)),
                  mod_spec, mod_spec,
                  pl.BlockSpec((d, ZW), lambda i, j: (0, 0), pipeline_mode=pl.Buffered(1)),
                  rope_spec, rope_spec, rope_spec],
        out_specs=pl.BlockSpec((tm, ZT), lambda i, j: (i, j)),
        out_shape=jax.ShapeDtypeStruct((n, ZW), F32),
        scratch_shapes=[pltpu.VMEM((tm, d), MXU_DTYPE)],
        compiler_params=_cparams(2, 48),
        name="in_proj",
    )(x, g, sc, sh, w, rc, rs1, rs2)


def _lru_kernel(xl_ref, yl_ref, stc_ref, sth_ref, cw_ref, cb_ref, wa_ref, ba_ref, wx_ref, bx_ref,
                lam_ref, o_ref, hlast_ref, convout_ref, xbuf, hcar, *, tt):
    ti = pl.program_id(1)
    w1 = CONV_W - 1
    c = xl_ref.shape[1]

    @pl.when(ti == 0)
    def _():
        xbuf[SUBLANES - w1:SUBLANES, :] = stc_ref[0]
        hcar[...] = sth_ref[0]

    @pl.when(ti > 0)
    def _():
        xbuf[0:SUBLANES, :] = xbuf[tt:tt + SUBLANES, :]

    x = xl_ref[...]
    xbuf[SUBLANES:SUBLANES + tt, :] = x
    xc = cb_ref[...] + cw_ref[w1:w1 + 1, :] * x
    for jj in range(w1):
        k = w1 - jj
        xc = xc + cw_ref[jj:jj + 1, :] * xbuf[SUBLANES - k:SUBLANES - k + tt, :]

    xcb = xc.astype(MXU_DTYPE)
    ra, ri = [], []
    for q in range(c // MXU_TILE):
        blk = xcb[:, MXU_TILE * q:MXU_TILE * (q + 1)]
        ra.append(_dot(blk, wa_ref[q]))
        ri.append(_dot(blk, wx_ref[q]))
    r = _sigmoid(jnp.concatenate(ra, axis=1) + ba_ref[...])
    i = _sigmoid(jnp.concatenate(ri, axis=1) + bx_ref[...])
    nl = -lam_ref[...]
    softplus = jnp.maximum(nl, 0.0) + jnp.log(1.0 + jnp.exp(-jnp.abs(nl)))
    log_a = (-LRU_C) * r * softplus
    a = jnp.exp(log_a)
    u = jnp.sqrt(1.0 - jnp.exp(2.0 * log_a)) * i * xc

    row = lax.broadcasted_iota(jnp.int32, (tt, c), 0)
    s = 1
    while s < tt:
        if s < SUBLANES:
            a_sh = jnp.where(row >= s, pltpu.roll(a, s, 0), 1.0)
            u_sh = jnp.where(row >= s, pltpu.roll(u, s, 0), 0.0)
        else:
            a_sh = jnp.concatenate([jnp.ones((s, c), F32), a[:tt - s]], axis=0)
            u_sh = jnp.concatenate([jnp.zeros((s, c), F32), u[:tt - s]], axis=0)
        u = u + a * u_sh
        a = a * a_sh
        s *= 2
    h = a * hcar[...] + u
    hcar[...] = h[tt - 1:tt]
    hlast_ref[0] = h[tt - 1:tt]
    convout_ref[0] = xbuf[SUBLANES + tt - w1:SUBLANES + tt, :]
    o_ref[...] = (h * _gelu_tanh(yl_ref[...])).astype(o_ref.dtype)


def _lru(z, st_conv, st_h, cw, cb, wa_bd, ba, wx_bd, bx, lam, *, nb, t, tt):
    c = cw.shape[1]
    nt = t // tt
    const2 = lambda b, i: (0, 0)
    const3 = lambda b, i: (0, 0, 0)
    return pl.pallas_call(
        functools.partial(_lru_kernel, tt=tt),
        grid=(nb, nt),
        in_specs=[pl.BlockSpec((tt, c), lambda b, i: (b * nt + i, COL_XL // c)),
                  pl.BlockSpec((tt, c), lambda b, i: (b * nt + i, COL_YL // c)),
                  pl.BlockSpec((1, CONV_W - 1, c), lambda b, i: (b, 0, 0)),
                  pl.BlockSpec((1, 1, c), lambda b, i: (b, 0, 0)),
                  pl.BlockSpec((CONV_W, c), const2),
                  pl.BlockSpec((1, c), const2),
                  pl.BlockSpec(wa_bd.shape, const3),
                  pl.BlockSpec((1, c), const2),
                  pl.BlockSpec(wx_bd.shape, const3),
                  pl.BlockSpec((1, c), const2),
                  pl.BlockSpec((1, c), const2)],
        out_specs=[pl.BlockSpec((tt, c), lambda b, i: (b * nt + i, 0)),
                   pl.BlockSpec((1, 1, c), lambda b, i: (b, 0, 0)),
                   pl.BlockSpec((1, CONV_W - 1, c), lambda b, i: (b, 0, 0))],
        out_shape=[jax.ShapeDtypeStruct((nb * t, c), MXU_DTYPE),
                   jax.ShapeDtypeStruct((nb, 1, c), F32),
                   jax.ShapeDtypeStruct((nb, CONV_W - 1, c), F32)],
        scratch_shapes=[pltpu.VMEM((tt + SUBLANES, c), F32), pltpu.VMEM((1, c), F32)],
        compiler_params=_cparams(2, 48),
        name="rg_lru",
    )(z, z, st_conv, st_h, cw, cb, wa_bd, ba, wx_bd, bx, lam)


def _diff_lambda(lp, lam_init):
    a = jnp.sum(lp[0:1] * lp[1:2], axis=1, keepdims=True)
    b = jnp.sum(lp[2:3] * lp[3:4], axis=1, keepdims=True)
    return jnp.exp(a) - jnp.exp(b) + lam_init


def _online_update(carry, s, pv):
    m, l, acc = carry
    m_new = jnp.maximum(m, jnp.max(s, axis=1, keepdims=True))
    alpha = jnp.exp(m - m_new)
    p = jnp.exp(s - m_new)
    l = alpha * l + jnp.sum(p, axis=1, keepdims=True)
    acc = alpha * acc + pv(p.astype(MXU_DTYPE))
    return m_new, l, acc


def _sort_key(score):
    bits = pltpu.bitcast(score, jnp.int32)
    return jnp.where(bits < 0, bits ^ jnp.int32(0x7FFFFFFF), bits)


def _topk_select(key, n_sel, tri):
    rows, length = key.shape
    int_min = jnp.int32(-2 ** 31)

    def count_ge(t):
        return jnp.sum(jnp.where(key >= t, 1.0, 0.0), axis=1, keepdims=True)

    ans = jnp.where(count_ge(jnp.zeros((rows, 1), jnp.int32)) >= n_sel, jnp.int32(0), int_min)

    def body(it, ans):
        t = ans | lax.shift_left(jnp.int32(1), jnp.int32(30) - it)
        return jnp.where(count_ge(t) >= n_sel, t, ans)

    thr = lax.fori_loop(0, 31, body, ans)
    gt = jnp.where(key > thr, 1.0, 0.0)
    eq = jnp.where(key == thr, 1.0, 0.0)
    need = n_sel - jnp.sum(gt, axis=1, keepdims=True)
    run = jnp.zeros((rows, 1), F32)
    out = []
    for c in range(length // LANES):
        eqc = eq[:, LANES * c:LANES * (c + 1)]
        before = _dot(eqc.astype(MXU_DTYPE), tri) + run
        out.append(gt[:, LANES * c:LANES * (c + 1)] + eqc * jnp.where(before < need, 1.0, 0.0))
        run = run + jnp.sum(eqc, axis=1, keepdims=True)
    return jnp.concatenate(out, axis=1)


def _diff_attn_kernel(q_ref, k_ref, v_ref, lp_ref, g_ref, o_ref, kb, vb, *, tq, lam_init):
    qi = pl.program_id(2)

    @pl.when(qi == 0)
    def _():
        kb[...] = k_ref[...].astype(MXU_DTYPE)
        vb[...] = v_ref[...].astype(MXU_DTYPE)

    q = q_ref[...] * (DK_DIFF ** -0.5)
    lane = lax.broadcasted_iota(jnp.int32, q.shape, 1)
    qq = jnp.concatenate([jnp.where(lane < DK_DIFF, q, 0.0), jnp.where(lane >= DK_DIFF, q, 0.0)],
                         axis=0).astype(MXU_DTYPE)

    lam = _diff_lambda(lp_ref[0], lam_init)

    def attend(length):
        s = _dot_nt(qq, kb[0:length, :])
        r = lax.broadcasted_iota(jnp.int32, s.shape, 0)
        qpos = qi * tq + jnp.where(r >= tq, r - tq, r)
        s = jnp.where(lax.broadcasted_iota(jnp.int32, s.shape, 1) <= qpos, s, NEG_BIG)
        p = jnp.exp(s - jnp.max(s, axis=1, keepdims=True))
        on = _dot(p.astype(MXU_DTYPE), vb[0:length, :]) / jnp.sum(p, axis=1, keepdims=True)
        o = on[:tq] - lam * on[tq:]
        o_ref[...] = (_rms(o, g_ref[...]) * (1.0 - lam_init)).astype(o_ref.dtype)

    span = max(CAUSAL_SPAN, tq)
    for v in range(kb.shape[0] // span):
        pl.when((qi * tq) // span == v)(functools.partial(attend, (v + 1) * span))


def _diff_attn_prompt(z, lp, g, *, nb, t, tq, lam_init):
    nq = t // tq
    hw = 2 * DK_DIFF
    return pl.pallas_call(
        functools.partial(_diff_attn_kernel, tq=tq, lam_init=lam_init),
        grid=(nb, H_DIFF, nq),
        in_specs=[pl.BlockSpec((tq, hw), lambda b, h, i: (b * nq + i, COL_QD // hw + h)),
                  pl.BlockSpec((t, hw), lambda b, h, i: (b, COL_KD // hw + h)),
                  pl.BlockSpec((t, DV_DIFF), lambda b, h, i: (b, COL_VD // DV_DIFF + h)),
                  pl.BlockSpec((1, 4, DK_DIFF), lambda b, h, i: (0, 0, 0)),
                  pl.BlockSpec((1, DV_DIFF), lambda b, h, i: (0, 0))],
        out_specs=pl.BlockSpec((tq, DV_DIFF), lambda b, h, i: (b * nq + i, h)),
        out_shape=jax.ShapeDtypeStruct((nb * t, H_DIFF * DV_DIFF), MXU_DTYPE),
        scratch_shapes=[pltpu.VMEM((t, hw), MXU_DTYPE), pltpu.VMEM((t, DV_DIFF), MXU_DTYPE)],
        compiler_params=_cparams(3, 40),
        name="diff_attn_prompt",
    )(z, z, z, lp, g)


def _dsa_kernel(qs_ref, ks_ref, vs_ref, qi_ref, ki_ref, wi_ref, tri_ref, o_ref, ksb, vsb, kib,
                *, tq, n_sel):
    qt = pl.program_id(1)

    @pl.when(qt == 0)
    def _():
        ksb[...] = ks_ref[...].astype(MXU_DTYPE)
        vsb[...] = vs_ref[...].astype(MXU_DTYPE)
        kib[...] = ki_ref[...].astype(MXU_DTYPE)

    def select_and_attend(length):
        lane = lax.broadcasted_iota(jnp.int32, (tq, LANES), 1)
        halves = (lane < D_IDX, lane >= D_IDX)

        w = wi_ref[...] * (H_IDX ** -0.5 * D_IDX ** -0.5)
        ki = kib[0:length, :]
        score = jnp.zeros((tq, length), F32)
        for m in range(H_IDX // 2):
            ch = qi_ref[:, LANES * m:LANES * (m + 1)]
            for half in range(2):
                d = _dot_nt(jnp.where(halves[half], ch, 0.0).astype(MXU_DTYPE), ki)
                hh = 2 * m + half
                score = score + w[:, hh:hh + 1] * jnp.maximum(d, 0.0)

        col = lax.broadcasted_iota(jnp.int32, (tq, length), 1)
        qpos = qt * tq + lax.broadcasted_iota(jnp.int32, (tq, length), 0)
        valid = col <= qpos
        key = _sort_key(jnp.where(valid, score, -jnp.inf))
        sel = _topk_select(key, n_sel, tri_ref[...]) * jnp.where(valid, 1.0, 0.0)
        keep = sel > 0.5

        outs = []
        for m in range(H_DSA // 2):
            qc = qs_ref[:, LANES * m:LANES * (m + 1)] * (DH_DSA ** -0.5)
            kc = ksb[0:length, LANES * m:LANES * (m + 1)]
            vc = vsb[0:length, LANES * m:LANES * (m + 1)]
            pair = []
            for half in range(2):
                s = _dot_nt(jnp.where(halves[half], qc, 0.0).astype(MXU_DTYPE), kc)
                s = jnp.where(keep, s, NEG_BIG)
                p = jnp.exp(s - jnp.max(s, axis=1, keepdims=True))
                l = jnp.sum(p, axis=1, keepdims=True)
                pair.append(_dot(p.astype(MXU_DTYPE), vc) / l)
            outs.append(jnp.where(halves[0], pair[0], pair[1]))
        o_ref[...] = jnp.concatenate(outs, axis=1).astype(o_ref.dtype)

    span = max(CAUSAL_SPAN, tq)
    for v in range(ks_ref.shape[0] // span):
        pl.when((qt * tq) // span == v)(functools.partial(select_and_attend, (v + 1) * span))


def _dsa_prompt(z, tri, *, nb, t, tq):
    nq = t // tq
    w = H_DSA * DH_DSA
    n_sel = min(TOPK_MAX, t // 4)
    assert max(CAUSAL_SPAN, tq) >= n_sel and t % max(CAUSAL_SPAN, tq) == 0
    return pl.pallas_call(
        functools.partial(_dsa_kernel, tq=tq, n_sel=n_sel),
        grid=(nb, nq),
        in_specs=[pl.BlockSpec((tq, w), lambda b, i: (b * nq + i, COL_QS // w)),
                  pl.BlockSpec((t, w), lambda b, i: (b, COL_KS // w)),
                  pl.BlockSpec((t, w), lambda b, i: (b, COL_VS // w)),
                  pl.BlockSpec((tq, w), lambda b, i: (b * nq + i, COL_QI // w)),
                  pl.BlockSpec((t, LANES), lambda b, i: (b, COL_SM // LANES)),
                  pl.BlockSpec((tq, LANES), lambda b, i: (b * nq + i, COL_SM // LANES + 1)),
                  pl.BlockSpec((LANES, LANES), lambda b, i: (0, 0))],
        out_specs=pl.BlockSpec((tq, w), lambda b, i: (b * nq + i, 0)),
        out_shape=jax.ShapeDtypeStruct((nb * t, w), MXU_DTYPE),
        scratch_shapes=[pltpu.VMEM((t, w), MXU_DTYPE), pltpu.VMEM((t, w), MXU_DTYPE),
                        pltpu.VMEM((t, LANES), MXU_DTYPE)],
        compiler_params=_cparams(2, 56),
        name="dsa_prompt",
    )(z, z, z, z, z, z, tri)


def _sample_select_kernel(pt_ref, qi_ref, wi_ref, kn_ref, tri_ref, *rest, pp, n_pages, n_sel, t):
    pages = rest[:pp]
    sel_ref, score, qrows, wcol = rest[pp:]
    p = pl.program_id(1)
    rows = H_IDX * t

    @pl.when(p == 0)
    def _():
        q = qi_ref[...]
        qrows[...] = jnp.concatenate([q[:, D_IDX * h:D_IDX * (h + 1)] for h in range(H_IDX)],
                                     axis=0).astype(MXU_DTYPE)
        w = wi_ref[...] * (H_IDX ** -0.5 * D_IDX ** -0.5)
        wcol[...] = jnp.concatenate([w[:, h:h + 1] for h in range(H_IDX)], axis=0)

    def page_scores(d):
        d = wcol[...] * jnp.maximum(d, 0.0)
        tot = d[0:t]
        for h in range(1, H_IDX):
            tot = tot + d[h * t:(h + 1) * t]
        return tot

    for r in range(pp):
        off = pl.multiple_of((p * pp + r) * LANES, LANES)
        score[:, pl.ds(off, LANES)] = page_scores(_dot(qrows[...], pages[r][...].astype(MXU_DTYPE)))

    @pl.when(p == n_pages // pp - 1)
    def _():
        new = page_scores(_dot_nt(qrows[...], kn_ref[...].astype(MXU_DTYPE)))
        ci = lax.broadcasted_iota(jnp.int32, (t, LANES), 1)
        ri = lax.broadcasted_iota(jnp.int32, (t, LANES), 0)
        score[:, n_pages * LANES:] = jnp.where(ci <= ri, new, -jnp.inf)
        sel_ref[...] = _topk_select(_sort_key(score[...]), n_sel, tri_ref[...])


def _sample_select(z, cache_idx, page_table, tri, kn_pad, *, layer, nb, t, pp):
    n_pages = page_table.shape[1]
    page = cache_idx.shape[3]
    assert page == LANES
    lpad = n_pages * page + LANES
    n_sel = min(TOPK_MAX, (n_pages * page + t) // 4)
    w = H_IDX * D_IDX
    page_specs = [pl.BlockSpec((None, None, D_IDX, page),
                               functools.partial(lambda b, p, pt, r: (layer, pt[b, p * pp + r], 0, 0), r=r))
                  for r in range(pp)]
    grid_spec = pltpu.PrefetchScalarGridSpec(
        num_scalar_prefetch=1,
        grid=(nb, n_pages // pp),
        in_specs=[pl.BlockSpec((t, w), lambda b, p, pt: (b, COL_QI // w)),
                  pl.BlockSpec((t, LANES), lambda b, p, pt: (b, COL_SM // LANES + 1)),
                  pl.BlockSpec((None, LANES, D_IDX), lambda b, p, pt: (b, 0, 0)),
                  pl.BlockSpec((LANES, LANES), lambda b, p, pt: (0, 0))] + page_specs,
        out_specs=pl.BlockSpec((None, t, lpad), lambda b, p, pt: (b, 0, 0)),
        scratch_shapes=[pltpu.VMEM((t, lpad), F32), pltpu.VMEM((H_IDX * t, D_IDX), MXU_DTYPE),
                        pltpu.VMEM((H_IDX * t, 1), F32)])
    return pl.pallas_call(
        functools.partial(_sample_select_kernel, pp=pp, n_pages=n_pages, n_sel=n_sel, t=t),
        grid_spec=grid_spec,
        out_shape=jax.ShapeDtypeStruct((nb, t, lpad), F32),
        compiler_params=_cparams(2, 32),
        name="sample_select",
    )(page_table, z, z, kn_pad, tri, *([cache_idx] * pp))


def _paged_attn_kernel(pt_ref, q_ref, kn_ref, vn_ref, *rest, pp, n_pages, t, mode, lam_init):
    if mode == "dsa":
        sel_ref, rest = rest[0], rest[1:]
    else:
        lp_ref, g_ref, rest = rest[0], rest[1], rest[2:]
    kpages, vpages = rest[:pp], rest[pp:2 * pp]
    o_ref, qrows, m_s, l_s, acc_s = rest[2 * pp:]
    p = pl.program_id(1)
    rows, qw = qrows.shape
    groups = rows // t

    @pl.when(p == 0)
    def _():
        if mode == "dsa":
            q = jnp.concatenate([q_ref[...] * (DH_DSA ** -0.5)] * groups, axis=0)
            grp = lax.broadcasted_iota(jnp.int32, (rows, qw), 0) // t
            lane_grp = lax.broadcasted_iota(jnp.int32, (rows, qw), 1) // DH_DSA
            qrows[...] = jnp.where(lane_grp == grp, q, 0.0).astype(MXU_DTYPE)
        else:
            q = q_ref[...] * (DK_DIFF ** -0.5)
            lane = lax.broadcasted_iota(jnp.int32, (t, qw), 1)
            parts = []
            for c in range(2):
                keep = (lane < DK_DIFF) if c == 0 else (lane >= DK_DIFF)
                for h in range(H_DIFF):
                    parts.append(jnp.where(keep, q[:, qw * h:qw * (h + 1)], 0.0))
            qrows[...] = jnp.concatenate(parts, axis=0).astype(MXU_DTYPE)
        m_s[...] = jnp.full(m_s.shape, NEG_BIG, F32)
        l_s[...] = jnp.zeros(l_s.shape, F32)
        acc_s[...] = jnp.zeros(acc_s.shape, F32)

    def update(s, pv):
        m, l, acc = _online_update((m_s[...], l_s[...], acc_s[...]), s, pv)
        m_s[...] = m
        l_s[...] = l
        acc_s[...] = acc

    def row_head(shape):
        return (lax.broadcasted_iota(jnp.int32, shape, 0) // t) % H_DIFF

    if mode == "dsa":
        kt = jnp.concatenate([r[...] for r in kpages], axis=1).astype(MXU_DTYPE)
        vt = jnp.concatenate([r[...] for r in vpages], axis=1).astype(MXU_DTYPE)
        s = _dot(qrows[...], kt)
        off = pl.multiple_of(p * (pp * LANES), pp * LANES)
        selc = sel_ref[:, pl.ds(off, pp * LANES)]
        s = jnp.where(jnp.concatenate([selc] * groups, axis=0) > 0.5, s, NEG_BIG)
        update(s, lambda pr: _dot_nt(pr, vt))
    else:
        ks = jnp.concatenate([r[...] for r in kpages], axis=0).astype(MXU_DTYPE)
        vs = jnp.concatenate([r[...] for r in vpages], axis=0).astype(MXU_DTYPE)
        s = _dot_nt(qrows[...], ks)
        col_head = lax.broadcasted_iota(jnp.int32, s.shape, 1) % H_DIFF
        s = jnp.where(col_head == row_head(s.shape), s, NEG_BIG)
        update(s, lambda pr: _dot(pr, vs))

    @pl.when(p == n_pages // pp - 1)
    def _():
        kn = kn_ref[...].astype(MXU_DTYPE)
        vn = vn_ref[...].astype(MXU_DTYPE)
        s = _dot_nt(qrows[...], kn)
        ci = lax.broadcasted_iota(jnp.int32, s.shape, 1)
        ti = lax.broadcasted_iota(jnp.int32, s.shape, 0) % t
        if mode == "dsa":
            seln = sel_ref[:, n_pages * LANES:]
            keep = (ci <= ti) & (jnp.concatenate([seln] * groups, axis=0) > 0.5)
        else:
            keep = (ci // H_DIFF <= ti) & (ci % H_DIFF == row_head(s.shape))
        update(jnp.where(keep, s, NEG_BIG), lambda pr: _dot(pr, vn))
        on = acc_s[...] / l_s[...]
        if mode == "dsa":
            lane = lax.broadcasted_iota(jnp.int32, (t, qw), 1)
            out = on[0:t]
            for h in range(1, H_DSA):
                out = jnp.where(lane // DH_DSA == h, on[h * t:(h + 1) * t], out)
            o_ref[...] = out.astype(o_ref.dtype)
        else:
            half = H_DIFF * t
            o = on[:half] - _diff_lambda(lp_ref[0], lam_init) * on[half:]
            o = _rms(o, g_ref[...]) * (1.0 - lam_init)
            o_ref[...] = jnp.concatenate([o[h * t:(h + 1) * t] for h in range(H_DIFF)],
                                         axis=1).astype(o_ref.dtype)


def _paged_attn(z, cache_k, cache_v, page_table, kn_pad, vn_pad, extra, *, layer, nb, t, pp, mode,
                q_col, lam_init=0.0):
    n_pages = page_table.shape[1]
    prow, pcol = cache_k.shape[2], cache_k.shape[3]
    if mode == "dsa":
        rows, qw, width, accw = H_DSA * t, H_DSA * DH_DSA, H_DSA * DH_DSA, H_DSA * DH_DSA
    else:
        rows, qw, width, accw = 2 * H_DIFF * t, 2 * DK_DIFF, H_DIFF * DV_DIFF, DV_DIFF

    def page_spec(r):
        return pl.BlockSpec((None, None, prow, pcol),
                            functools.partial(lambda b, p, pt, r: (layer, pt[b, p * pp + r], 0, 0), r=r))

    new_spec = pl.BlockSpec((None, LANES, kn_pad.shape[2]), lambda b, p, pt: (b, 0, 0))
    in_specs = [pl.BlockSpec((t, width), lambda b, p, pt: (b, q_col // width)), new_spec, new_spec]
    if mode == "dsa":
        (sel,) = extra
        in_specs.append(pl.BlockSpec((None, t, sel.shape[2]), lambda b, p, pt: (b, 0, 0)))
    else:
        in_specs += [pl.BlockSpec((1, 4, DK_DIFF), lambda b, p, pt: (0, 0, 0)),
                     pl.BlockSpec((1, DV_DIFF), lambda b, p, pt: (0, 0))]
    in_specs += [page_spec(r) for r in range(pp)] * 2
    grid_spec = pltpu.PrefetchScalarGridSpec(
        num_scalar_prefetch=1,
        grid=(nb, n_pages // pp),
        in_specs=in_specs,
        out_specs=pl.BlockSpec((t, width), lambda b, p, pt: (b, 0)),
        scratch_shapes=[pltpu.VMEM((rows, qw), MXU_DTYPE), pltpu.VMEM((rows, 1), F32),
                        pltpu.VMEM((rows, 1), F32), pltpu.VMEM((rows, accw), F32)])
    return pl.pallas_call(
        functools.partial(_paged_attn_kernel, pp=pp, n_pages=n_pages, t=t, mode=mode, lam_init=lam_init),
        grid_spec=grid_spec,
        out_shape=jax.ShapeDtypeStruct((nb * t, width), MXU_DTYPE),
        compiler_params=_cparams(2, 48),
        name="paged_attn_" + mode,
    )(page_table, z, kn_pad, vn_pad, *extra, *([cache_k] * pp), *([cache_v] * pp))


def _merge_kernel(ol_ref, od_ref, os_ref, zg_ref, x_ref, ga_ref, wl_ref, wd_ref, ws_ref, wo_ref,
                  g_ref, o_ref):
    d = x_ref.shape[1]
    zg = zg_ref[...]
    merged = (_sigmoid(zg[:, 0:d]) * _dot(ol_ref[...], wl_ref[...])
              + _sigmoid(zg[:, d:2 * d]) * _dot(od_ref[...], wd_ref[...])
              + _sigmoid(zg[:, 2 * d:3 * d]) * _dot(os_ref[...], ws_ref[...]))
    y = _dot(merged.astype(MXU_DTYPE), wo_ref[...])
    o_ref[...] = x_ref[...] + ga_ref[0] * _rms(y, g_ref[...])


def _merge(o_lru, od, os_, z, x, ga, wl, wd, ws, wo, g, *, tm, tiles_per_mod):
    n, d = x.shape
    r = ga.shape[1]
    row = lambda i: (i, 0)
    const = lambda i: (0, 0)
    return pl.pallas_call(
        _merge_kernel,
        grid=(n // tm,),
        in_specs=[pl.BlockSpec((tm, o_lru.shape[1]), row),
                  pl.BlockSpec((tm, od.shape[1]), row),
                  pl.BlockSpec((tm, os_.shape[1]), row),
                  pl.BlockSpec((tm, 3 * d), lambda i: (i, COL_G // (3 * d))),
                  pl.BlockSpec((tm, d), row),
                  pl.BlockSpec((1, r, d), lambda i: (i // tiles_per_mod, 0, 0)),
                  pl.BlockSpec(wl.shape, const), pl.BlockSpec(wd.shape, const),
                  pl.BlockSpec(ws.shape, const), pl.BlockSpec(wo.shape, const),
                  pl.BlockSpec((1, d), const)],
        out_specs=pl.BlockSpec((tm, d), row),
        out_shape=jax.ShapeDtypeStruct((n, d), F32),
        compiler_params=_cparams(1, 48),
        name="merge_out",
    )(o_lru, od, os_, z, x, ga, wl, wd, ws, wo, g)


HALO = 16


def _ffn_up_kernel(x_ref, xh_ref, g_ref, sc_ref, sh_ref, w_ref, cwa_ref, cwb_ref, cba_ref,
                   cbb_ref, sta_ref, stb_ref, f_ref, keepa_ref, keepb_ref, h_s, ua_s, ub_s,
                   *, tm, tn, seq, keep):
    i = pl.program_id(0)
    j = pl.program_id(1)
    w1 = FFN_CONV_W - 1
    long_seq = seq >= tm
    dff = w_ref.shape[1] // 2

    @pl.when(j == 0)
    def _():
        def normed(xv):
            return (_rms(xv, g_ref[...]) * (1.0 + sc_ref[0]) + sh_ref[0]).astype(h_s.dtype)
        if long_seq:
            h_s[0:HALO, :] = normed(xh_ref[...])
        else:
            h_s[0:HALO, :] = jnp.zeros((HALO, h_s.shape[1]), h_s.dtype)
        h_s[HALO:, :] = normed(x_ref[...])

    def half(col0, cw_ref, cb_ref, st_ref, keep_ref, u_s):
        w = w_ref[:, pl.ds(pl.multiple_of(col0 + j * tn, tn), tn)]
        u_s[...] = _dot(h_s[...], w)
        if long_seq:
            first = i % (seq // tm) == 0
            u_s[HALO - w1:HALO, :] = jnp.where(first, st_ref[0], u_s[HALO - w1:HALO, :])
        u = u_s[HALO:, :]
        keep_ref[...] = u_s[HALO + tm - keep:, :]
        y = cb_ref[...] + cw_ref[w1:w1 + 1, :] * u
        if not long_seq:
            tpos = lax.broadcasted_iota(jnp.int32, u.shape, 0) % seq
        for jj in range(w1):
            k = w1 - jj
            prev = u_s[HALO - k:HALO - k + tm, :]
            if not long_seq:
                prev = jnp.where(tpos >= k, prev, st_ref[jj])
            y = y + cw_ref[jj:jj + 1, :] * prev
        return y

    ya = half(0, cwa_ref, cba_ref, sta_ref, keepa_ref, ua_s)
    yb = half(dff, cwb_ref, cbb_ref, stb_ref, keepb_ref, ub_s)
    f_ref[...] = (_gelu_tanh(ya) * yb).astype(f_ref.dtype)


def _ffn_up(x, g, sc, sh, w_up, cw, cb, st, *, tm, tn, seq, tiles_per_mod):
    n, d = x.shape
    dff = w_up.shape[1] // 2
    nj = dff // tn
    r = sc.shape[1]
    long_seq = seq >= tm
    keep = SUBLANES if long_seq else tm
    mod_spec = pl.BlockSpec((1, r, d), lambda i, j: (i // tiles_per_mod, 0, 0))
    if long_seq:
        tps = seq // tm
        sta = pl.BlockSpec((1, FFN_CONV_W - 1, tn), lambda i, j: (i // tps, 0, j))
        stb = pl.BlockSpec((1, FFN_CONV_W - 1, tn), lambda i, j: (i // tps, 0, nj + j))
    else:
        sta = pl.BlockSpec((FFN_CONV_W - 1, tm, tn), lambda i, j: (0, i, j))
        stb = pl.BlockSpec((FFN_CONV_W - 1, tm, tn), lambda i, j: (0, i, nj + j))
    n_keep = (n // tm) * keep
    return pl.pallas_call(
        functools.partial(_ffn_up_kernel, tm=tm, tn=tn, seq=seq, keep=keep),
        grid=(n // tm, nj),
        in_specs=[pl.BlockSpec((tm, d), lambda i, j: (i, 0)),
                  pl.BlockSpec((HALO, d), lambda i, j: (jnp.maximum(i * (tm // HALO) - 1, 0), 0)),
                  pl.BlockSpec((1, d), lambda i, j: (0, 0)),
                  mod_spec, mod_spec,
                  pl.BlockSpec((d, 2 * dff), lambda i, j: (0, 0), pipeline_mode=pl.Buffered(1)),
                  pl.BlockSpec((FFN_CONV_W, tn), lambda i, j: (0, j)),
                  pl.BlockSpec((FFN_CONV_W, tn), lambda i, j: (0, nj + j)),
                  pl.BlockSpec((1, tn), lambda i, j: (0, j)),
                  pl.BlockSpec((1, tn), lambda i, j: (0, nj + j)),
                  sta, stb],
        out_specs=[pl.BlockSpec((tm, tn), lambda i, j: (i, j)),
                   pl.BlockSpec((keep, tn), lambda i, j: (i, j)),
                   pl.BlockSpec((keep, tn), lambda i, j: (i, j))],
        out_shape=[jax.ShapeDtypeStruct((n, dff), MXU_DTYPE),
                   jax.ShapeDtypeStruct((n_keep, dff), F32),
                   jax.ShapeDtypeStruct((n_keep, dff), F32)],
        scratch_shapes=[pltpu.VMEM((HALO + tm, d), MXU_DTYPE), pltpu.VMEM((HALO + tm, tn), F32),
                        pltpu.VMEM((HALO + tm, tn), F32)],
        compiler_params=_cparams(2, 48),
        name="ffn_up",
    )(x, x, g, sc, sh, w_up, cw, cw, cb, cb, st, st)


def _ffn_down_kernel(f_ref, w_ref, x_ref, ga_ref, g_ref, o_ref):
    y = _dot(f_ref[...], w_ref[...])
    o_ref[...] = x_ref[...] + ga_ref[0] * _rms(y, g_ref[...])


def _ffn_down(f, w, x, ga, g, *, tm, tiles_per_mod):
    n, d = x.shape
    r = ga.shape[1]
    return pl.pallas_call(
        _ffn_down_kernel,
        grid=(n // tm,),
        in_specs=[pl.BlockSpec((tm, f.shape[1]), lambda i: (i, 0)),
                  pl.BlockSpec(w.shape, lambda i: (0, 0)),
                  pl.BlockSpec((tm, d), lambda i: (i, 0)),
                  pl.BlockSpec((1, r, d), lambda i: (i // tiles_per_mod, 0, 0)),
                  pl.BlockSpec((1, d), lambda i: (0, 0))],
        out_specs=pl.BlockSpec((tm, d), lambda i: (i, 0)),
        out_shape=jax.ShapeDtypeStruct((n, d), F32),
        compiler_params=_cparams(1, 48),
        name="ffn_down",
    )(f, w, x, ga, g)


def _rope_tables(pos):
    rot = D_IDX // 4
    half = rot // 2
    inv = ROPE_THETA ** (-jnp.arange(half, dtype=F32) * (2.0 / rot))
    ang = pos.astype(F32)[:, None] * inv[None, :]
    cos, sin = jnp.cos(ang), jnp.sin(ang)
    p = pos.shape[0]
    one = jnp.ones((p, D_IDX - rot), F32)
    zero = jnp.zeros((p, D_IDX - rot), F32)
    zh = jnp.zeros((p, half), F32)
    c = jnp.concatenate([cos, cos, one], axis=1)
    s1 = jnp.concatenate([-sin, zh, zero], axis=1)
    s2 = jnp.concatenate([zh, sin, zero], axis=1)
    return tuple(jnp.tile(a, (1, LANES // D_IDX)) for a in (c, s1, s2))


def _pack_w_in(w_in):
    depth, d, _ = w_in.shape
    n_main = COL_SM
    ki = w_in[:, :, n_main:n_main + D_IDX]
    wi = w_in[:, :, n_main + D_IDX:n_main + D_IDX + H_IDX]
    gate = w_in[:, :, n_main + D_IDX + H_IDX:]
    pad = jnp.zeros((depth, d, COL_G - COL_SM - 2 * D_IDX - H_IDX), w_in.dtype)
    return jnp.concatenate([w_in[:, :, :n_main], ki, ki, wi, pad, gate], axis=2).astype(MXU_DTYPE)


def _block_diag(w, per):
    depth, nblk, k, _ = w.shape
    w = w.reshape(depth, nblk // per, per, k, k)
    eye = jnp.eye(per, dtype=w.dtype)
    return jnp.einsum("dgpij,pq->dgpiqj", w, eye).reshape(depth, nblk // per, per * k, per * k)


def _pad_rows(a, nb, t):
    a = a.reshape(nb, t, a.shape[1])
    return jnp.pad(a, ((0, 0), (0, LANES - t), (0, 0)))


def kernel(x_prompt, x_sample, cache_diff_k, cache_diff_v, cache_dsa_k, cache_dsa_v, cache_idx_k, state_lru_h, state_lru_conv, state_ffn_conv, page_table, c_prompt, c_sample, w_ada, b_ada, g_pre_mix, g_post_mix, g_pre_ffn, g_post_ffn, w_in, lru_conv_w, lru_conv_b, lru_wa, lru_ba, lru_wx, lru_bx, lru_lambda, diff_lambda, diff_subln_g, w_branch, w_out, w_up, ffn_conv_w, ffn_conv_b, w_down):
    bp, tp, d = x_prompt.shape
    bs, ts, _ = x_sample.shape
    depth = w_in.shape[0]
    d_rnn = lru_conv_w.shape[2]
    dff2 = w_up.shape[2]
    n_pool, page = cache_diff_k.shape[1], cache_diff_k.shape[2]
    past_len = page_table.shape[1] * page
    w_diff, w_dsa = H_DIFF * DV_DIFF, H_DSA * DH_DSA
    np_, ns_ = bp * tp, bs * ts

    w_in_p = _pack_w_in(w_in)
    wa_bd = _block_diag(lru_wa, MXU_TILE // (d_rnn // N_RNN_BLOCKS)).astype(MXU_DTYPE)
    wx_bd = _block_diag(lru_wx, MXU_TILE // (d_rnn // N_RNN_BLOCKS)).astype(MXU_DTYPE)
    wb = w_branch.astype(MXU_DTYPE)
    wo = w_out.astype(MXU_DTYPE)
    wu = w_up.astype(MXU_DTYPE)
    wd = w_down.astype(MXU_DTYPE)
    tri = jnp.triu(jnp.ones((LANES, LANES), F32), k=1).astype(MXU_DTYPE)
    row2 = lambda a: a[:, None, :]

    mod = _ada(jnp.concatenate([c_prompt, c_sample], axis=0), w_ada, b_ada)

    rope_p = _rope_tables(jnp.arange(tp, dtype=jnp.int32))
    pos_s = past_len + jnp.arange(ts, dtype=jnp.int32)
    rope_s = _rope_tables(jnp.tile(pos_s, bs))

    cdk = cache_diff_k.reshape(depth, n_pool, page * H_DIFF, 2 * DK_DIFF)
    cdv = cache_diff_v.reshape(depth, n_pool, page * H_DIFF, DV_DIFF)
    csk = cache_dsa_k.transpose(0, 1, 3, 4, 2).reshape(depth, n_pool, w_dsa, page)
    csv = cache_dsa_v.transpose(0, 1, 3, 4, 2).reshape(depth, n_pool, w_dsa, page)
    cik = cache_idx_k.transpose(0, 1, 3, 2)

    tm_p, tq_p, tt_p = 512, 256, 256
    xp = x_prompt.reshape(np_, d)
    xs = x_sample.reshape(ns_, d)
    outs_p, outs_s = [], []
    zeros_conv = jnp.zeros((bp, CONV_W - 1, d_rnn), F32)
    zeros_h = jnp.zeros((bp, 1, d_rnn), F32)
    zeros_ffn = jnp.zeros((bp, FFN_CONV_W - 1, dff2), F32)

    for l in range(depth):
        lam_init = 0.8 - 0.6 * math.exp(-0.3 * l)
        m = mod[l]
        chunks = [m[:, k * d:(k + 1) * d] for k in range(6)]
        mp = [c[:bp][:, None, :] for c in chunks]
        ms = [jnp.repeat(c[bp:], ts, axis=0)[None] for c in chunks]
        lru_w = (lru_conv_w[l], row2(lru_conv_b)[l], wa_bd[l], row2(lru_ba)[l], wx_bd[l],
                 row2(lru_bx)[l], row2(lru_lambda)[l])
        lp = diff_lambda[l][None]
        gsub = row2(diff_subln_g)[l]
        wl_, wd_, ws_ = wb[l, :d_rnn], wb[l, d_rnn:d_rnn + w_diff], wb[l, d_rnn + w_diff:]

        z = _in_proj(xp, row2(g_pre_mix)[l], mp[1], mp[0], w_in_p[l], rope_p,
                     tm=tm_p, tiles_per_mod=tp // tm_p, rope_tiles=tp // tm_p)
        o_lru, h_last, conv_new = _lru(z, zeros_conv, zeros_h, *lru_w, nb=bp, t=tp, tt=tt_p)
        od = _diff_attn_prompt(z, lp, gsub, nb=bp, t=tp, tq=tq_p, lam_init=lam_init)
        os_ = _dsa_prompt(z, tri, nb=bp, t=tp, tq=tq_p)
        x1 = _merge(o_lru, od, os_, z, xp, mp[2], wl_, wd_, ws_, wo[l], row2(g_post_mix)[l],
                    tm=256, tiles_per_mod=tp // 256)
        f, keep_a, keep_b = _ffn_up(x1, row2(g_pre_ffn)[l], mp[4], mp[3], wu[l], ffn_conv_w[l],
                                    row2(ffn_conv_b)[l], zeros_ffn, tm=tm_p, tn=256, seq=tp,
                                    tiles_per_mod=tp // tm_p)
        xp = _ffn_down(f, wd[l], x1, mp[5], row2(g_post_ffn)[l], tm=256, tiles_per_mod=tp // 256)
        keep_u = jnp.concatenate([keep_a, keep_b], axis=1).reshape(bp, tp // tm_p, SUBLANES, dff2)
        outs_p.append((z[:, COL_KD:COL_KD + w_diff].reshape(bp, tp, H_DIFF, 2 * DK_DIFF),
                       z[:, COL_VD:COL_VD + w_diff].reshape(bp, tp, H_DIFF, DV_DIFF),
                       z[:, COL_KS:COL_KS + w_dsa].reshape(bp, tp, H_DSA, DH_DSA),
                       z[:, COL_VS:COL_VS + w_dsa].reshape(bp, tp, H_DSA, DH_DSA),
                       z[:, COL_SM:COL_SM + D_IDX].reshape(bp, tp, D_IDX),
                       h_last[:, 0], conv_new,
                       keep_u[:, -1, SUBLANES - (FFN_CONV_W - 1):]))

        z = _in_proj(xs, row2(g_pre_mix)[l], ms[1], ms[0], w_in_p[l], rope_s,
                     tm=ns_, tiles_per_mod=1, rope_tiles=1)
        o_lru, h_last, conv_new = _lru(z, state_lru_conv[l], state_lru_h[l][:, None, :], *lru_w,
                                       nb=bs, t=ts, tt=ts)
        kn_i = _pad_rows(z[:, COL_SM:COL_SM + D_IDX], bs, ts)
        sel = _sample_select(z, cik, page_table, tri, kn_i, layer=l, nb=bs, t=ts, pp=8)
        os_ = _paged_attn(z, csk, csv, page_table, _pad_rows(z[:, COL_KS:COL_KS + w_dsa], bs, ts),
                          _pad_rows(z[:, COL_VS:COL_VS + w_dsa], bs, ts), (sel,), layer=l, nb=bs,
                          t=ts, pp=8, mode="dsa", q_col=COL_QS)
        kn_d = z[:, COL_KD:COL_KD + w_diff].reshape(ns_ * H_DIFF, 2 * DK_DIFF)
        vn_d = z[:, COL_VD:COL_VD + w_diff].reshape(ns_ * H_DIFF, DV_DIFF)
        od = _paged_attn(z, cdk, cdv, page_table, _pad_rows(kn_d, bs, ts * H_DIFF),
                         _pad_rows(vn_d, bs, ts * H_DIFF), (lp, gsub), layer=l,
                         nb=bs, t=ts, pp=8, mode="diff", q_col=COL_QD, lam_init=lam_init)
        x1 = _merge(o_lru, od, os_, z, xs, ms[2], wl_, wd_, ws_, wo[l], row2(g_post_mix)[l],
                    tm=ns_, tiles_per_mod=1)
        stf = state_ffn_conv[l]
        st_rows = jnp.stack([
            jnp.pad(stf[:, jj:, :], ((0, 0), (0, ts - (FFN_CONV_W - 1 - jj)), (0, 0))).reshape(ns_, dff2)
            for jj in range(FFN_CONV_W - 1)])
        f, keep_a, keep_b = _ffn_up(x1, row2(g_pre_ffn)[l], ms[4], ms[3], wu[l], ffn_conv_w[l],
                                    row2(ffn_conv_b)[l], st_rows, tm=ns_, tn=256, seq=ts,
                                    tiles_per_mod=1)
        xs = _ffn_down(f, wd[l], x1, ms[5], row2(g_post_ffn)[l], tm=ns_, tiles_per_mod=1)
        u_all = jnp.concatenate([keep_a, keep_b], axis=1).reshape(bs, ts, dff2)
        outs_s.append((z[:, COL_KD:COL_KD + w_diff].reshape(bs, ts, H_DIFF, 2 * DK_DIFF),
                       z[:, COL_VD:COL_VD + w_diff].reshape(bs, ts, H_DIFF, DV_DIFF),
                       z[:, COL_KS:COL_KS + w_dsa].reshape(bs, ts, H_DSA, DH_DSA),
                       z[:, COL_VS:COL_VS + w_dsa].reshape(bs, ts, H_DSA, DH_DSA),
                       z[:, COL_SM:COL_SM + D_IDX].reshape(bs, ts, D_IDX),
                       h_last[:, 0], conv_new,
                       u_all[:, ts - (FFN_CONV_W - 1):]))

    stack = lambda rows: [jnp.stack([r[k] for r in rows]) for k in range(8)]
    return (xp.reshape(bp, tp, d), xs.reshape(bs, ts, d), *stack(outs_p), *stack(outs_s))
```

```python
import functools
import math

import jax
import jax.numpy as jnp
from jax import lax
from jax.experimental import pallas as pl
from jax.experimental.pallas import tpu as pltpu

F32 = jnp.float32
MXU_DTYPE = jnp.bfloat16

N_RNN_BLOCKS = 16
CONV_W = 4
LRU_C = 8.0
H_DIFF, DK_DIFF = 4, 64
DV_DIFF = 2 * DK_DIFF
H_DSA, DH_DSA = 8, 64
H_IDX, D_IDX = 8, 64
TOPK_MAX = 256
ROPE_THETA = 500000.0
FFN_CONV_W = 3
EPS = 1e-6
NEG_BIG = -1e30

LANES = 128
SUBLANES = 8
MXU_TILE = 256
MIB = 1024 * 1024
CAUSAL_SPAN = 512

ZT = 512
COL_XL, COL_YL = 0, 1024
COL_QD, COL_KD, COL_VD = 2048, 2560, 3072
COL_QS, COL_KS, COL_VS = 3584, 4096, 4608
COL_QI, COL_SM, COL_G = 5120, 5632, 6144
ZW = 9216
ROPE_FULL_TILES = tuple(c // ZT for c in (COL_QD, COL_KD, COL_QS, COL_KS, COL_QI))
ROPE_HEAD_TILE = COL_SM // ZT


def _cparams(n_axes, vmem_mib):
    return pltpu.CompilerParams(dimension_semantics=("arbitrary",) * n_axes,
                                vmem_limit_bytes=vmem_mib * MIB)


def _sigmoid(x):
    return 1.0 / (1.0 + jnp.exp(-x))


def _gelu_tanh(x):
    return 0.5 * x * (1.0 + jnp.tanh(math.sqrt(2.0 / math.pi) * (x + 0.044715 * (x * x * x))))


def _dot(a, b):
    return jnp.dot(a, b, preferred_element_type=F32)


def _dot_nt(a, b):
    return lax.dot_general(a, b, (((1,), (1,)), ((), ())), preferred_element_type=F32)


def _rms(x, g):
    return x * lax.rsqrt(jnp.mean(x * x, axis=-1, keepdims=True) + EPS) * g


def _ada_kernel(c_ref, w_ref, b_ref, o_ref):
    c = c_ref[...]
    a = (c * _sigmoid(c)).astype(MXU_DTYPE)
    o_ref[...] = _dot(a, w_ref[...].astype(MXU_DTYPE)) + b_ref[...]


def _ada(c_all, w_ada, b_ada):
    depth, d, w6 = w_ada.shape
    rows = c_all.shape[0]
    tn = 1536
    return pl.pallas_call(
        _ada_kernel,
        grid=(depth, w6 // tn),
        in_specs=[pl.BlockSpec((rows, d), lambda l, j: (0, 0)),
                  pl.BlockSpec((None, d, tn), lambda l, j: (l, 0, j)),
                  pl.BlockSpec((None, 1, tn), lambda l, j: (l, 0, j))],
        out_specs=pl.BlockSpec((None, rows, tn), lambda l, j: (l, 0, j)),
        out_shape=jax.ShapeDtypeStruct((depth, rows, w6), F32),
        compiler_params=_cparams(2, 40),
        name="ada_mod",
    )(c_all, w_ada, b_ada.reshape(depth, 1, w6))


def _in_proj_kernel(x_ref, g_ref, sc_ref, sh_ref, w_ref, rc_ref, rs1_ref, rs2_ref, z_ref, h_ref):
    j = pl.program_id(1)

    @pl.when(j == 0)
    def _():
        y = _rms(x_ref[...], g_ref[...])
        h_ref[...] = (y * (1.0 + sc_ref[0]) + sh_ref[0]).astype(h_ref.dtype)

    acc = _dot(h_ref[...], w_ref[:, pl.ds(pl.multiple_of(j * ZT, ZT), ZT)])

    def roped(n_chunks):
        outs = []
        for c in range(ZT // LANES):
            ch = acc[:, LANES * c:LANES * (c + 1)]
            if c < n_chunks:
                ch = (ch * rc_ref[...] + pltpu.roll(ch, LANES - 8, 1) * rs1_ref[...]
                      + pltpu.roll(ch, 8, 1) * rs2_ref[...])
            outs.append(ch)
        return jnp.concatenate(outs, axis=1)

    is_full = j == ROPE_FULL_TILES[0]
    for t in ROPE_FULL_TILES[1:]:
        is_full = is_full | (j == t)
    is_head = j == ROPE_HEAD_TILE

    @pl.when(is_full)
    def _():
        z_ref[...] = roped(ZT // LANES)

    @pl.when(is_head)
    def _():
        z_ref[...] = roped(1)

    @pl.when(jnp.logical_not(is_full | is_head))
    def _():
        z_ref[...] = acc


def _in_proj(x, g, sc, sh, w, rope, *, tm, tiles_per_mod, rope_tiles):
    n, d = x.shape
    r = sc.shape[1]
    rc, rs1, rs2 = rope
    mod_spec = pl.BlockSpec((1, r, d), lambda i, j: (i // tiles_per_mod, 0, 0))
    rope_spec = pl.BlockSpec((tm, LANES), lambda i, j: (i % rope_tiles, 0))
    return pl.pallas_call(
        _in_proj_kernel,
        grid=(n // tm, ZW // ZT),
        in_specs=[pl.BlockSpec((tm, d), lambda i, j: (i, 0)),
                  pl.BlockSpec((1, d), lambda i, j: (0, 0)),
                  mod_spec, mod_spec,
                  pl.BlockSpec((d, ZW), lambda i, j: (0, 0), pipeline_mode=pl.Buffered(1)),
                  rope_spec, rope_spec, rope_spec],
        out_specs=pl.BlockSpec((tm, ZT), lambda i, j: (i, j)),
        out_shape=jax.ShapeDtypeStruct((n, ZW), F32),
        scratch_shapes=[pltpu.VMEM((tm, d), MXU_DTYPE)],
        compiler_params=_cparams(2, 48),
        name="in_proj",
    )(x, g, sc, sh, w, rc, rs1, rs2)


def _lru_kernel(xl_ref, yl_ref, stc_ref, sth_ref, cw_ref, cb_ref, wa_ref, ba_ref, wx_ref, bx_ref,
                lam_ref, o_ref, hlast_ref, convout_ref, xbuf, hcar, *, tt):
    ti = pl.program_id(1)
    w1 = CONV_W - 1
    c = xl_ref.shape[1]

    @pl.when(ti == 0)
    def _():
        xbuf[SUBLANES - w1:SUBLANES, :] = stc_ref[0]
        hcar[...] = sth_ref[0]

    @pl.when(ti > 0)
    def _():
        xbuf[0:SUBLANES, :] = xbuf[tt:tt + SUBLANES, :]

    x = xl_ref[...]
    xbuf[SUBLANES:SUBLANES + tt, :] = x
    xc = cb_ref[...] + cw_ref[w1:w1 + 1, :] * x
    for jj in range(w1):
        k = w1 - jj
        xc = xc + cw_ref[jj:jj + 1, :] * xbuf[SUBLANES - k:SUBLANES - k + tt, :]

    xcb = xc.astype(MXU_DTYPE)
    ra, ri = [], []
    for q in range(c // MXU_TILE):
        blk = xcb[:, MXU_TILE * q:MXU_TILE * (q + 1)]
        ra.append(_dot(blk, wa_ref[q]))
        ri.append(_dot(blk, wx_ref[q]))
    r = _sigmoid(jnp.concatenate(ra, axis=1) + ba_ref[...])
    i = _sigmoid(jnp.concatenate(ri, axis=1) + bx_ref[...])
    nl = -lam_ref[...]
    softplus = jnp.maximum(nl, 0.0) + jnp.log(1.0 + jnp.exp(-jnp.abs(nl)))
    log_a = (-LRU_C) * r * softplus
    a = jnp.exp(log_a)
    u = jnp.sqrt(1.0 - jnp.exp(2.0 * log_a)) * i * xc

    row = lax.broadcasted_iota(jnp.int32, (tt, c), 0)
    s = 1
    while s < tt:
        if s < SUBLANES:
            a_sh = jnp.where(row >= s, pltpu.roll(a, s, 0), 1.0)
            u_sh = jnp.where(row >= s, pltpu.roll(u, s, 0), 0.0)
        else:
            a_sh = jnp.concatenate([jnp.ones((s, c), F32), a[:tt - s]], axis=0)
            u_sh = jnp.concatenate([jnp.zeros((s, c), F32), u[:tt - s]], axis=0)
        u = u + a * u_sh
        a = a * a_sh
        s *= 2
    h = a * hcar[...] + u
    hcar[...] = h[tt - 1:tt]
    hlast_ref[0] = h[tt - 1:tt]
    convout_ref[0] = xbuf[SUBLANES + tt - w1:SUBLANES + tt, :]
    o_ref[...] = (h * _gelu_tanh(yl_ref[...])).astype(o_ref.dtype)


def _lru(z, st_conv, st_h, cw, cb, wa_bd, ba, wx_bd, bx, lam, *, nb, t, tt):
    c = cw.shape[1]
    nt = t // tt
    const2 = lambda b, i: (0, 0)
    const3 = lambda b, i: (0, 0, 0)
    return pl.pallas_call(
        functools.partial(_lru_kernel, tt=tt),
        grid=(nb, nt),
        in_specs=[pl.BlockSpec((tt, c), lambda b, i: (b * nt + i, COL_XL // c)),
                  pl.BlockSpec((tt, c), lambda b, i: (b * nt + i, COL_YL // c)),
                  pl.BlockSpec((1, CONV_W - 1, c), lambda b, i: (b, 0, 0)),
                  pl.BlockSpec((1, 1, c), lambda b, i: (b, 0, 0)),
                  pl.BlockSpec((CONV_W, c), const2),
                  pl.BlockSpec((1, c), const2),
                  pl.BlockSpec(wa_bd.shape, const3),
                  pl.BlockSpec((1, c), const2),
                  pl.BlockSpec(wx_bd.shape, const3),
                  pl.BlockSpec((1, c), const2),
                  pl.BlockSpec((1, c), const2)],
        out_specs=[pl.BlockSpec((tt, c), lambda b, i: (b * nt + i, 0)),
                   pl.BlockSpec((1, 1, c), lambda b, i: (b, 0, 0)),
                   pl.BlockSpec((1, CONV_W - 1, c), lambda b, i: (b, 0, 0))],
        out_shape=[jax.ShapeDtypeStruct((nb * t, c), MXU_DTYPE),
                   jax.ShapeDtypeStruct((nb, 1, c), F32),
                   jax.ShapeDtypeStruct((nb, CONV_W - 1, c), F32)],
        scratch_shapes=[pltpu.VMEM((tt + SUBLANES, c), F32), pltpu.VMEM((1, c), F32)],
        compiler_params=_cparams(2, 48),
        name="rg_lru",
    )(z, z, st_conv, st_h, cw, cb, wa_bd, ba, wx_bd, bx, lam)


def _diff_lambda(lp, lam_init):
    a = jnp.sum(lp[0:1] * lp[1:2], axis=1, keepdims=True)
    b = jnp.sum(lp[2:3] * lp[3:4], axis=1, keepdims=True)
    return jnp.exp(a) - jnp.exp(b) + lam_init


def _online_update(carry, s, pv):
    m, l, acc = carry
    m_new = jnp.maximum(m, jnp.max(s, axis=1, keepdims=True))
    alpha = jnp.exp(m - m_new)
    p = jnp.exp(s - m_new)
    l = alpha * l + jnp.sum(p, axis=1, keepdims=True)
    acc = alpha * acc + pv(p.astype(MXU_DTYPE))
    return m_new, l, acc


def _sort_key(score):
    bits = pltpu.bitcast(score, jnp.int32)
    return jnp.where(bits < 0, bits ^ jnp.int32(0x7FFFFFFF), bits)


def _topk_select(key, n_sel, tri):
    rows, length = key.shape
    int_min = jnp.int32(-2 ** 31)

    def count_ge(t):
        return jnp.sum(jnp.where(key >= t, 1.0, 0.0), axis=1, keepdims=True)

    ans = jnp.where(count_ge(jnp.zeros((rows, 1), jnp.int32)) >= n_sel, jnp.int32(0), int_min)

    def body(it, ans):
        t = ans | lax.shift_left(jnp.int32(1), jnp.int32(30) - it)
        return jnp.where(count_ge(t) >= n_sel, t, ans)

    thr = lax.fori_loop(0, 31, body, ans)
    gt = jnp.where(key > thr, 1.0, 0.0)
    eq = jnp.where(key == thr, 1.0, 0.0)
    need = n_sel - jnp.sum(gt, axis=1, keepdims=True)
    run = jnp.zeros((rows, 1), F32)
    out = []
    for c in range(length // LANES):
        eqc = eq[:, LANES * c:LANES * (c + 1)]
        before = _dot(eqc.astype(MXU_DTYPE), tri) + run
        out.append(gt[:, LANES * c:LANES * (c + 1)] + eqc * jnp.where(before < need, 1.0, 0.0))
        run = run + jnp.sum(eqc, axis=1, keepdims=True)
    return jnp.concatenate(out, axis=1)


def _diff_attn_kernel(q_ref, k_ref, v_ref, lp_ref, g_ref, o_ref, kb, vb, s_s, *, tq, lam_init):
    qi = pl.program_id(2)

    @pl.when(qi == 0)
    def _():
        kb[...] = k_ref[...].astype(MXU_DTYPE)
        vb[...] = v_ref[...].astype(MXU_DTYPE)

    q = q_ref[...] * (DK_DIFF ** -0.5)
    lane = lax.broadcasted_iota(jnp.int32, q.shape, 1)
    qq = jnp.concatenate([jnp.where(lane < DK_DIFF, q, 0.0), jnp.where(lane >= DK_DIFF, q, 0.0)],
                         axis=0).astype(MXU_DTYPE)

    lam = _diff_lambda(lp_ref[0], lam_init)

    span = max(CAUSAL_SPAN, tq)

    def attend(length):
        chunks = [(c, c + span) for c in range(0, length, span)]
        m = jnp.full((2 * tq, 1), NEG_BIG, F32)
        for lo, hi in chunks:
            s = _dot_nt(qq, kb[lo:hi, :])
            if hi == length:
                r = lax.broadcasted_iota(jnp.int32, s.shape, 0)
                qpos = qi * tq + jnp.where(r >= tq, r - tq, r)
                s = jnp.where(lo + lax.broadcasted_iota(jnp.int32, s.shape, 1) <= qpos, s, NEG_BIG)
            s_s[:, lo:hi] = s
            m = jnp.maximum(m, jnp.max(s, axis=1, keepdims=True))
        l = jnp.zeros((2 * tq, 1), F32)
        acc = jnp.zeros((2 * tq, DV_DIFF), F32)
        for lo, hi in chunks:
            p = jnp.exp(s_s[:, lo:hi] - m)
            l = l + jnp.sum(p, axis=1, keepdims=True)
            acc = acc + _dot(p.astype(MXU_DTYPE), vb[lo:hi, :])
        on = acc / l
        o = on[:tq] - lam * on[tq:]
        o_ref[...] = (_rms(o, g_ref[...]) * (1.0 - lam_init)).astype(o_ref.dtype)

    for v in range(kb.shape[0] // span):
        pl.when((qi * tq) // span == v)(functools.partial(attend, (v + 1) * span))


def _diff_attn_prompt(z, lp, g, *, nb, t, tq, lam_init):
    nq = t // tq
    hw = 2 * DK_DIFF
    return pl.pallas_call(
        functools.partial(_diff_attn_kernel, tq=tq, lam_init=lam_init),
        grid=(nb, H_DIFF, nq),
        in_specs=[pl.BlockSpec((tq, hw), lambda b, h, i: (b * nq + i, COL_QD // hw + h)),
                  pl.BlockSpec((t, hw), lambda b, h, i: (b, COL_KD // hw + h)),
                  pl.BlockSpec((t, DV_DIFF), lambda b, h, i: (b, COL_VD // DV_DIFF + h)),
                  pl.BlockSpec((1, 4, DK_DIFF), lambda b, h, i: (0, 0, 0)),
                  pl.BlockSpec((1, DV_DIFF), lambda b, h, i: (0, 0))],
        out_specs=pl.BlockSpec((tq, DV_DIFF), lambda b, h, i: (b * nq + i, h)),
        out_shape=jax.ShapeDtypeStruct((nb * t, H_DIFF * DV_DIFF), MXU_DTYPE),
        scratch_shapes=[pltpu.VMEM((t, hw), MXU_DTYPE), pltpu.VMEM((t, DV_DIFF), MXU_DTYPE),
                        pltpu.VMEM((2 * tq, t), F32)],
        compiler_params=_cparams(3, 40),
        name="diff_attn_prompt",
    )(z, z, z, lp, g)


def _dsa_kernel(qs_ref, ks_ref, vs_ref, qi_ref, ki_ref, wi_ref, tri_ref, o_ref, ksb, vsb, kib,
                *, tq, n_sel):
    qt = pl.program_id(1)

    @pl.when(qt == 0)
    def _():
        ksb[...] = ks_ref[...].astype(MXU_DTYPE)
        vsb[...] = vs_ref[...].astype(MXU_DTYPE)
        kib[...] = ki_ref[...].astype(MXU_DTYPE)

    def select_and_attend(length):
        lane = lax.broadcasted_iota(jnp.int32, (tq, LANES), 1)
        halves = (lane < D_IDX, lane >= D_IDX)

        w = wi_ref[...] * (H_IDX ** -0.5 * D_IDX ** -0.5)
        ki = kib[0:length, :]
        score = jnp.zeros((tq, length), F32)
        for m in range(H_IDX // 2):
            ch = qi_ref[:, LANES * m:LANES * (m + 1)]
            for half in range(2):
                d = _dot_nt(jnp.where(halves[half], ch, 0.0).astype(MXU_DTYPE), ki)
                hh = 2 * m + half
                score = score + w[:, hh:hh + 1] * jnp.maximum(d, 0.0)

        col = lax.broadcasted_iota(jnp.int32, (tq, length), 1)
        qpos = qt * tq + lax.broadcasted_iota(jnp.int32, (tq, length), 0)
        valid = col <= qpos
        key = _sort_key(jnp.where(valid, score, -jnp.inf))
        sel = _topk_select(key, n_sel, tri_ref[...]) * jnp.where(valid, 1.0, 0.0)
        keep = sel > 0.5

        outs = []
        for m in range(H_DSA // 2):
            qc = qs_ref[:, LANES * m:LANES * (m + 1)] * (DH_DSA ** -0.5)
            kc = ksb[0:length, LANES * m:LANES * (m + 1)]
            vc = vsb[0:length, LANES * m:LANES * (m + 1)]
            pair = []
            for half in range(2):
                s = _dot_nt(jnp.where(halves[half], qc, 0.0).astype(MXU_DTYPE), kc)
                s = jnp.where(keep, s, NEG_BIG)
                p = jnp.exp(s - jnp.max(s, axis=1, keepdims=True))
                l = jnp.sum(p, axis=1, keepdims=True)
                pair.append(_dot(p.astype(MXU_DTYPE), vc) / l)
            outs.append(jnp.where(halves[0], pair[0], pair[1]))
        o_ref[...] = jnp.concatenate(outs, axis=1).astype(o_ref.dtype)

    span = max(CAUSAL_SPAN, tq)
    for v in range(ks_ref.shape[0] // span):
        pl.when((qt * tq) // span == v)(functools.partial(select_and_attend, (v + 1) * span))


def _dsa_prompt(z, tri, *, nb, t, tq):
    nq = t // tq
    w = H_DSA * DH_DSA
    n_sel = min(TOPK_MAX, t // 4)
    assert max(CAUSAL_SPAN, tq) >= n_sel and t % max(CAUSAL_SPAN, tq) == 0
    return pl.pallas_call(
        functools.partial(_dsa_kernel, tq=tq, n_sel=n_sel),
        grid=(nb, nq),
        in_specs=[pl.BlockSpec((tq, w), lambda b, i: (b * nq + i, COL_QS // w)),
                  pl.BlockSpec((t, w), lambda b, i: (b, COL_KS // w)),
                  pl.BlockSpec((t, w), lambda b, i: (b, COL_VS // w)),
                  pl.BlockSpec((tq, w), lambda b, i: (b * nq + i, COL_QI // w)),
                  pl.BlockSpec((t, LANES), lambda b, i: (b, COL_SM // LANES)),
                  pl.BlockSpec((tq, LANES), lambda b, i: (b * nq + i, COL_SM // LANES + 1)),
                  pl.BlockSpec((LANES, LANES), lambda b, i: (0, 0))],
        out_specs=pl.BlockSpec((tq, w), lambda b, i: (b * nq + i, 0)),
        out_shape=jax.ShapeDtypeStruct((nb * t, w), MXU_DTYPE),
        scratch_shapes=[pltpu.VMEM((t, w), MXU_DTYPE), pltpu.VMEM((t, w), MXU_DTYPE),
                        pltpu.VMEM((t, LANES), MXU_DTYPE)],
        compiler_params=_cparams(2, 56),
        name="dsa_prompt",
    )(z, z, z, z, z, z, tri)


def _sample_select_kernel(pt_ref, qi_ref, wi_ref, kn_ref, tri_ref, *rest, pp, n_pages, n_sel, t):
    pages = rest[:pp]
    sel_ref, score, qrows, wcol = rest[pp:]
    p = pl.program_id(1)
    rows = H_IDX * t

    @pl.when(p == 0)
    def _():
        q = qi_ref[...]
        qrows[...] = jnp.concatenate([q[:, D_IDX * h:D_IDX * (h + 1)] for h in range(H_IDX)],
                                     axis=0).astype(MXU_DTYPE)
        w = wi_ref[...] * (H_IDX ** -0.5 * D_IDX ** -0.5)
        wcol[...] = jnp.concatenate([w[:, h:h + 1] for h in range(H_IDX)], axis=0)

    def page_scores(d):
        d = wcol[...] * jnp.maximum(d, 0.0)
        tot = d[0:t]
        for h in range(1, H_IDX):
            tot = tot + d[h * t:(h + 1) * t]
        return tot

    for r in range(pp):
        off = pl.multiple_of((p * pp + r) * LANES, LANES)
        score[:, pl.ds(off, LANES)] = page_scores(_dot(qrows[...], pages[r][...].astype(MXU_DTYPE)))

    @pl.when(p == n_pages // pp - 1)
    def _():
        new = page_scores(_dot_nt(qrows[...], kn_ref[...].astype(MXU_DTYPE)))
        ci = lax.broadcasted_iota(jnp.int32, (t, LANES), 1)
        ri = lax.broadcasted_iota(jnp.int32, (t, LANES), 0)
        score[:, n_pages * LANES:] = jnp.where(ci <= ri, new, -jnp.inf)
        sel_ref[...] = _topk_select(_sort_key(score[...]), n_sel, tri_ref[...])


def _sample_select(z, cache_idx, page_table, tri, kn_pad, *, layer, nb, t, pp):
    n_pages = page_table.shape[1]
    page = cache_idx.shape[3]
    assert page == LANES
    lpad = n_pages * page + LANES
    n_sel = min(TOPK_MAX, (n_pages * page + t) // 4)
    w = H_IDX * D_IDX
    page_specs = [pl.BlockSpec((None, None, D_IDX, page),
                               functools.partial(lambda b, p, pt, r: (layer, pt[b, p * pp + r], 0, 0), r=r))
                  for r in range(pp)]
    grid_spec = pltpu.PrefetchScalarGridSpec(
        num_scalar_prefetch=1,
        grid=(nb, n_pages // pp),
        in_specs=[pl.BlockSpec((t, w), lambda b, p, pt: (b, COL_QI // w)),
                  pl.BlockSpec((t, LANES), lambda b, p, pt: (b, COL_SM // LANES + 1)),
                  pl.BlockSpec((None, LANES, D_IDX), lambda b, p, pt: (b, 0, 0)),
                  pl.BlockSpec((LANES, LANES), lambda b, p, pt: (0, 0))] + page_specs,
        out_specs=pl.BlockSpec((None, t, lpad), lambda b, p, pt: (b, 0, 0)),
        scratch_shapes=[pltpu.VMEM((t, lpad), F32), pltpu.VMEM((H_IDX * t, D_IDX), MXU_DTYPE),
                        pltpu.VMEM((H_IDX * t, 1), F32)])
    return pl.pallas_call(
        functools.partial(_sample_select_kernel, pp=pp, n_pages=n_pages, n_sel=n_sel, t=t),
        grid_spec=grid_spec,
        out_shape=jax.ShapeDtypeStruct((nb, t, lpad), F32),
        compiler_params=_cparams(2, 32),
        name="sample_select",
    )(page_table, z, z, kn_pad, tri, *([cache_idx] * pp))


def _paged_attn_kernel(pt_ref, q_ref, kn_ref, vn_ref, *rest, pp, n_pages, t, mode, lam_init):
    if mode == "dsa":
        sel_ref, rest = rest[0], rest[1:]
    else:
        lp_ref, g_ref, rest = rest[0], rest[1], rest[2:]
    kpages, vpages = rest[:pp], rest[pp:2 * pp]
    o_ref, qrows, m_s, l_s, acc_s = rest[2 * pp:]
    p = pl.program_id(1)
    rows, qw = qrows.shape
    groups = rows // t

    @pl.when(p == 0)
    def _():
        if mode == "dsa":
            q = jnp.concatenate([q_ref[...] * (DH_DSA ** -0.5)] * groups, axis=0)
            grp = lax.broadcasted_iota(jnp.int32, (rows, qw), 0) // t
            lane_grp = lax.broadcasted_iota(jnp.int32, (rows, qw), 1) // DH_DSA
            qrows[...] = jnp.where(lane_grp == grp, q, 0.0).astype(MXU_DTYPE)
        else:
            q = q_ref[...] * (DK_DIFF ** -0.5)
            lane = lax.broadcasted_iota(jnp.int32, (t, qw), 1)
            parts = []
            for c in range(2):
                keep = (lane < DK_DIFF) if c == 0 else (lane >= DK_DIFF)
                for h in range(H_DIFF):
                    parts.append(jnp.where(keep, q[:, qw * h:qw * (h + 1)], 0.0))
            qrows[...] = jnp.concatenate(parts, axis=0).astype(MXU_DTYPE)
        m_s[...] = jnp.full(m_s.shape, NEG_BIG, F32)
        l_s[...] = jnp.zeros(l_s.shape, F32)
        acc_s[...] = jnp.zeros(acc_s.shape, F32)

    def update(s, pv):
        m, l, acc = _online_update((m_s[...], l_s[...], acc_s[...]), s, pv)
        m_s[...] = m
        l_s[...] = l
        acc_s[...] = acc

    def row_head(shape):
        return (lax.broadcasted_iota(jnp.int32, shape, 0) // t) % H_DIFF

    if mode == "dsa":
        kt = jnp.concatenate([r[...] for r in kpages], axis=1).astype(MXU_DTYPE)
        vt = jnp.concatenate([r[...] for r in vpages], axis=1).astype(MXU_DTYPE)
        s = _dot(qrows[...], kt)
        off = pl.multiple_of(p * (pp * LANES), pp * LANES)
        selc = sel_ref[:, pl.ds(off, pp * LANES)]
        s = jnp.where(jnp.concatenate([selc] * groups, axis=0) > 0.5, s, NEG_BIG)
        update(s, lambda pr: _dot_nt(pr, vt))
    else:
        ks = jnp.concatenate([r[...] for r in kpages], axis=0).astype(MXU_DTYPE)
        vs = jnp.concatenate([r[...] for r in vpages], axis=0).astype(MXU_DTYPE)
        s = _dot_nt(qrows[...], ks)
        col_head = lax.broadcasted_iota(jnp.int32, s.shape, 1) % H_DIFF
        s = jnp.where(col_head == row_head(s.shape), s, NEG_BIG)
        update(s, lambda pr: _dot(pr, vs))

    @pl.when(p == n_pages // pp - 1)
    def _():
        kn = kn_ref[...].astype(MXU_DTYPE)
        vn = vn_ref[...].astype(MXU_DTYPE)
        s = _dot_nt(qrows[...], kn)
        ci = lax.broadcasted_iota(jnp.int32, s.shape, 1)
        ti = lax.broadcasted_iota(jnp.int32, s.shape, 0) % t
        if mode == "dsa":
            seln = sel_ref[:, n_pages * LANES:]
            keep = (ci <= ti) & (jnp.concatenate([seln] * groups, axis=0) > 0.5)
        else:
            keep = (ci // H_DIFF <= ti) & (ci % H_DIFF == row_head(s.shape))
        update(jnp.where(keep, s, NEG_BIG), lambda pr: _dot(pr, vn))
        on = acc_s[...] / l_s[...]
        if mode == "dsa":
            lane = lax.broadcasted_iota(jnp.int32, (t, qw), 1)
            out = on[0:t]
            for h in range(1, H_DSA):
                out = jnp.where(lane // DH_DSA == h, on[h * t:(h + 1) * t], out)
            o_ref[...] = out.astype(o_ref.dtype)
        else:
            half = H_DIFF * t
            o = on[:half] - _diff_lambda(lp_ref[0], lam_init) * on[half:]
            o = _rms(o, g_ref[...]) * (1.0 - lam_init)
            o_ref[...] = jnp.concatenate([o[h * t:(h + 1) * t] for h in range(H_DIFF)],
                                         axis=1).astype(o_ref.dtype)


def _paged_attn(z, cache_k, cache_v, page_table, kn_pad, vn_pad, extra, *, layer, nb, t, pp, mode,
                q_col, lam_init=0.0):
    n_pages = page_table.shape[1]
    prow, pcol = cache_k.shape[2], cache_k.shape[3]
    if mode == "dsa":
        rows, qw, width, accw = H_DSA * t, H_DSA * DH_DSA, H_DSA * DH_DSA, H_DSA * DH_DSA
    else:
        rows, qw, width, accw = 2 * H_DIFF * t, 2 * DK_DIFF, H_DIFF * DV_DIFF, DV_DIFF

    def page_spec(r):
        return pl.BlockSpec((None, None, prow, pcol),
                            functools.partial(lambda b, p, pt, r: (layer, pt[b, p * pp + r], 0, 0), r=r))

    new_spec = pl.BlockSpec((None, LANES, kn_pad.shape[2]), lambda b, p, pt: (b, 0, 0))
    in_specs = [pl.BlockSpec((t, width), lambda b, p, pt: (b, q_col // width)), new_spec, new_spec]
    if mode == "dsa":
        (sel,) = extra
        in_specs.append(pl.BlockSpec((None, t, sel.shape[2]), lambda b, p, pt: (b, 0, 0)))
    else:
        in_specs += [pl.BlockSpec((1, 4, DK_DIFF), lambda b, p, pt: (0, 0, 0)),
                     pl.BlockSpec((1, DV_DIFF), lambda b, p, pt: (0, 0))]
    in_specs += [page_spec(r) for r in range(pp)] * 2
    grid_spec = pltpu.PrefetchScalarGridSpec(
        num_scalar_prefetch=1,
        grid=(nb, n_pages // pp),
        in_specs=in_specs,
        out_specs=pl.BlockSpec((t, width), lambda b, p, pt: (b, 0)),
        scratch_shapes=[pltpu.VMEM((rows, qw), MXU_DTYPE), pltpu.VMEM((rows, 1), F32),
                        pltpu.VMEM((rows, 1), F32), pltpu.VMEM((rows, accw), F32)])
    return pl.pallas_call(
        functools.partial(_paged_attn_kernel, pp=pp, n_pages=n_pages, t=t, mode=mode, lam_init=lam_init),
        grid_spec=grid_spec,
        out_shape=jax.ShapeDtypeStruct((nb * t, width), MXU_DTYPE),
        compiler_params=_cparams(2, 48),
        name="paged_attn_" + mode,
    )(page_table, z, kn_pad, vn_pad, *extra, *([cache_k] * pp), *([cache_v] * pp))


def _merge_kernel(ol_ref, od_ref, os_ref, zg_ref, x_ref, ga_ref, wl_ref, wd_ref, ws_ref, wo_ref,
                  g_ref, o_ref):
    d = x_ref.shape[1]
    zg = zg_ref[...]
    merged = (_sigmoid(zg[:, 0:d]) * _dot(ol_ref[...], wl_ref[...])
              + _sigmoid(zg[:, d:2 * d]) * _dot(od_ref[...], wd_ref[...])
              + _sigmoid(zg[:, 2 * d:3 * d]) * _dot(os_ref[...], ws_ref[...]))
    y = _dot(merged.astype(MXU_DTYPE), wo_ref[...])
    o_ref[...] = x_ref[...] + ga_ref[0] * _rms(y, g_ref[...])


def _merge(o_lru, od, os_, z, x, ga, wl, wd, ws, wo, g, *, tm, tiles_per_mod):
    n, d = x.shape
    r = ga.shape[1]
    row = lambda i: (i, 0)
    const = lambda i: (0, 0)
    return pl.pallas_call(
        _merge_kernel,
        grid=(n // tm,),
        in_specs=[pl.BlockSpec((tm, o_lru.shape[1]), row),
                  pl.BlockSpec((tm, od.shape[1]), row),
                  pl.BlockSpec((tm, os_.shape[1]), row),
                  pl.BlockSpec((tm, 3 * d), lambda i: (i, COL_G // (3 * d))),
                  pl.BlockSpec((tm, d), row),
                  pl.BlockSpec((1, r, d), lambda i: (i // tiles_per_mod, 0, 0)),
                  pl.BlockSpec(wl.shape, const), pl.BlockSpec(wd.shape, const),
                  pl.BlockSpec(ws.shape, const), pl.BlockSpec(wo.shape, const),
                  pl.BlockSpec((1, d), const)],
        out_specs=pl.BlockSpec((tm, d), row),
        out_shape=jax.ShapeDtypeStruct((n, d), F32),
        compiler_params=_cparams(1, 48),
        name="merge_out",
    )(o_lru, od, os_, z, x, ga, wl, wd, ws, wo, g)


HALO = 16


def _ffn_up_kernel(x_ref, xh_ref, g_ref, sc_ref, sh_ref, w_ref, cwa_ref, cwb_ref, cba_ref,
                   cbb_ref, sta_ref, stb_ref, f_ref, keepa_ref, keepb_ref, h_s, ua_s, ub_s,
                   *, tm, tn, seq, keep):
    i = pl.program_id(0)
    j = pl.program_id(1)
    w1 = FFN_CONV_W - 1
    long_seq = seq >= tm
    dff = w_ref.shape[1] // 2

    @pl.when(j == 0)
    def _():
        def normed(xv):
            return (_rms(xv, g_ref[...]) * (1.0 + sc_ref[0]) + sh_ref[0]).astype(h_s.dtype)
        if long_seq:
            h_s[0:HALO, :] = normed(xh_ref[...])
        else:
            h_s[0:HALO, :] = jnp.zeros((HALO, h_s.shape[1]), h_s.dtype)
        h_s[HALO:, :] = normed(x_ref[...])

    def half(col0, cw_ref, cb_ref, st_ref, keep_ref, u_s):
        w = w_ref[:, pl.ds(pl.multiple_of(col0 + j * tn, tn), tn)]
        u_s[...] = _dot(h_s[...], w)
        if long_seq:
            first = i % (seq // tm) == 0
            u_s[HALO - w1:HALO, :] = jnp.where(first, st_ref[0], u_s[HALO - w1:HALO, :])
        u = u_s[HALO:, :]
        keep_ref[...] = u_s[HALO + tm - keep:, :]
        y = cb_ref[...] + cw_ref[w1:w1 + 1, :] * u
        if not long_seq:
            tpos = lax.broadcasted_iota(jnp.int32, u.shape, 0) % seq
        for jj in range(w1):
            k = w1 - jj
            prev = u_s[HALO - k:HALO - k + tm, :]
            if not long_seq:
                prev = jnp.where(tpos >= k, prev, st_ref[jj])
            y = y + cw_ref[jj:jj + 1, :] * prev
        return y

    ya = half(0, cwa_ref, cba_ref, sta_ref, keepa_ref, ua_s)
    yb = half(dff, cwb_ref, cbb_ref, stb_ref, keepb_ref, ub_s)
    f_ref[...] = (_gelu_tanh(ya) * yb).astype(f_ref.dtype)


def _ffn_up(x, g, sc, sh, w_up, cw, cb, st, *, tm, tn, seq, tiles_per_mod):
    n, d = x.shape
    dff = w_up.shape[1] // 2
    nj = dff // tn
    r = sc.shape[1]
    long_seq = seq >= tm
    keep = SUBLANES if long_seq else tm
    mod_spec = pl.BlockSpec((1, r, d), lambda i, j: (i // tiles_per_mod, 0, 0))
    if long_seq:
        tps = seq // tm
        sta = pl.BlockSpec((1, FFN_CONV_W - 1, tn), lambda i, j: (i // tps, 0, j))
        stb = pl.BlockSpec((1, FFN_CONV_W - 1, tn), lambda i, j: (i // tps, 0, nj + j))
    else:
        sta = pl.BlockSpec((FFN_CONV_W - 1, tm, tn), lambda i, j: (0, i, j))
        stb = pl.BlockSpec((FFN_CONV_W - 1, tm, tn), lambda i, j: (0, i, nj + j))
    n_keep = (n // tm) * keep
    return pl.pallas_call(
        functools.partial(_ffn_up_kernel, tm=tm, tn=tn, seq=seq, keep=keep),
        grid=(n // tm, nj),
        in_specs=[pl.BlockSpec((tm, d), lambda i, j: (i, 0)),
                  pl.BlockSpec((HALO, d), lambda i, j: (jnp.maximum(i * (tm // HALO) - 1, 0), 0)),
                  pl.BlockSpec((1, d), lambda i, j: (0, 0)),
                  mod_spec, mod_spec,
                  pl.BlockSpec((d, 2 * dff), lambda i, j: (0, 0), pipeline_mode=pl.Buffered(1)),
                  pl.BlockSpec((FFN_CONV_W, tn), lambda i, j: (0, j)),
                  pl.BlockSpec((FFN_CONV_W, tn), lambda i, j: (0, nj + j)),
                  pl.BlockSpec((1, tn), lambda i, j: (0, j)),
                  pl.BlockSpec((1, tn), lambda i, j: (0, nj + j)),
                  sta, stb],
        out_specs=[pl.BlockSpec((tm, tn), lambda i, j: (i, j)),
                   pl.BlockSpec((keep, tn), lambda i, j: (i, j)),
                   pl.BlockSpec((keep, tn), lambda i, j: (i, j))],
        out_shape=[jax.ShapeDtypeStruct((n, dff), MXU_DTYPE),
                   jax.ShapeDtypeStruct((n_keep, dff), F32),
                   jax.ShapeDtypeStruct((n_keep, dff), F32)],
        scratch_shapes=[pltpu.VMEM((HALO + tm, d), MXU_DTYPE), pltpu.VMEM((HALO + tm, tn), F32),
                        pltpu.VMEM((HALO + tm, tn), F32)],
        compiler_params=_cparams(2, 48),
        name="ffn_up",
    )(x, x, g, sc, sh, w_up, cw, cw, cb, cb, st, st)


def _ffn_down_kernel(f_ref, w_ref, x_ref, ga_ref, g_ref, o_ref):
    y = _dot(f_ref[...], w_ref[...])
    o_ref[...] = x_ref[...] + ga_ref[0] * _rms(y, g_ref[...])


def _ffn_down(f, w, x, ga, g, *, tm, tiles_per_mod):
    n, d = x.shape
    r = ga.shape[1]
    return pl.pallas_call(
        _ffn_down_kernel,
        grid=(n // tm,),
        in_specs=[pl.BlockSpec((tm, f.shape[1]), lambda i: (i, 0)),
                  pl.BlockSpec(w.shape, lambda i: (0, 0)),
                  pl.BlockSpec((tm, d), lambda i: (i, 0)),
                  pl.BlockSpec((1, r, d), lambda i: (i // tiles_per_mod, 0, 0)),
                  pl.BlockSpec((1, d), lambda i: (0, 0))],
        out_specs=pl.BlockSpec((tm, d), lambda i: (i, 0)),
        out_shape=jax.ShapeDtypeStruct((n, d), F32),
        compiler_params=_cparams(1, 48),
        name="ffn_down",
    )(f, w, x, ga, g)


def _rope_tables(pos):
    rot = D_IDX // 4
    half = rot // 2
    inv = ROPE_THETA ** (-jnp.arange(half, dtype=F32) * (2.0 / rot))
    ang = pos.astype(F32)[:, None] * inv[None, :]
    cos, sin = jnp.cos(ang), jnp.sin(ang)
    p = pos.shape[0]
    one = jnp.ones((p, D_IDX - rot), F32)
    zero = jnp.zeros((p, D_IDX - rot), F32)
    zh = jnp.zeros((p, half), F32)
    c = jnp.concatenate([cos, cos, one], axis=1)
    s1 = jnp.concatenate([-sin, zh, zero], axis=1)
    s2 = jnp.concatenate([zh, sin, zero], axis=1)
    return tuple(jnp.tile(a, (1, LANES // D_IDX)) for a in (c, s1, s2))


def _pack_w_in(w_in):
    depth, d, _ = w_in.shape
    n_main = COL_SM
    ki = w_in[:, :, n_main:n_main + D_IDX]
    wi = w_in[:, :, n_main + D_IDX:n_main + D_IDX + H_IDX]
    gate = w_in[:, :, n_main + D_IDX + H_IDX:]
    pad = jnp.zeros((depth, d, COL_G - COL_SM - 2 * D_IDX - H_IDX), w_in.dtype)
    return jnp.concatenate([w_in[:, :, :n_main], ki, ki, wi, pad, gate], axis=2).astype(MXU_DTYPE)


def _block_diag(w, per):
    depth, nblk, k, _ = w.shape
    w = w.reshape(depth, nblk // per, per, k, k)
    eye = jnp.eye(per, dtype=w.dtype)
    return jnp.einsum("dgpij,pq->dgpiqj", w, eye).reshape(depth, nblk // per, per * k, per * k)


def _pad_rows(a, nb, t):
    a = a.reshape(nb, t, a.shape[1])
    return jnp.pad(a, ((0, 0), (0, LANES - t), (0, 0)))


def kernel(x_prompt, x_sample, cache_diff_k, cache_diff_v, cache_dsa_k, cache_dsa_v, cache_idx_k, state_lru_h, state_lru_conv, state_ffn_conv, page_table, c_prompt, c_sample, w_ada, b_ada, g_pre_mix, g_post_mix, g_pre_ffn, g_post_ffn, w_in, lru_conv_w, lru_conv_b, lru_wa, lru_ba, lru_wx, lru_bx, lru_lambda, diff_lambda, diff_subln_g, w_branch, w_out, w_up, ffn_conv_w, ffn_conv_b, w_down):
    bp, tp, d = x_prompt.shape
    bs, ts, _ = x_sample.shape
    depth = w_in.shape[0]
    d_rnn = lru_conv_w.shape[2]
    dff2 = w_up.shape[2]
    n_pool, page = cache_diff_k.shape[1], cache_diff_k.shape[2]
    past_len = page_table.shape[1] * page
    w_diff, w_dsa = H_DIFF * DV_DIFF, H_DSA * DH_DSA
    np_, ns_ = bp * tp, bs * ts

    w_in_p = _pack_w_in(w_in)
    wa_bd = _block_diag(lru_wa, MXU_TILE // (d_rnn // N_RNN_BLOCKS)).astype(MXU_DTYPE)
    wx_bd = _block_diag(lru_wx, MXU_TILE // (d_rnn // N_RNN_BLOCKS)).astype(MXU_DTYPE)
    wb = w_branch.astype(MXU_DTYPE)
    wo = w_out.astype(MXU_DTYPE)
    wu = w_up.astype(MXU_DTYPE)
    wd = w_down.astype(MXU_DTYPE)
    tri = jnp.triu(jnp.ones((LANES, LANES), F32), k=1).astype(MXU_DTYPE)
    row2 = lambda a: a[:, None, :]

    mod = _ada(jnp.concatenate([c_prompt, c_sample], axis=0), w_ada, b_ada)

    rope_p = _rope_tables(jnp.arange(tp, dtype=jnp.int32))
    pos_s = past_len + jnp.arange(ts, dtype=jnp.int32)
    rope_s = _rope_tables(jnp.tile(pos_s, bs))

    cdk = cache_diff_k.reshape(depth, n_pool, page * H_DIFF, 2 * DK_DIFF)
    cdv = cache_diff_v.reshape(depth, n_pool, page * H_DIFF, DV_DIFF)
    csk = cache_dsa_k.transpose(0, 1, 3, 4, 2).reshape(depth, n_pool, w_dsa, page)
    csv = cache_dsa_v.transpose(0, 1, 3, 4, 2).reshape(depth, n_pool, w_dsa, page)
    cik = cache_idx_k.transpose(0, 1, 3, 2)

    tm_p, tq_p, tt_p = 512, 256, 256
    tm_in = min(1024, tp)
    pp_idx, pp_kv = 32, 16
    xp = x_prompt.reshape(np_, d)
    xs = x_sample.reshape(ns_, d)
    outs_p, outs_s = [], []
    zeros_conv = jnp.zeros((bp, CONV_W - 1, d_rnn), F32)
    zeros_h = jnp.zeros((bp, 1, d_rnn), F32)
    zeros_ffn = jnp.zeros((bp, FFN_CONV_W - 1, dff2), F32)

    for l in range(depth):
        lam_init = 0.8 - 0.6 * math.exp(-0.3 * l)
        m = mod[l]
        chunks = [m[:, k * d:(k + 1) * d] for k in range(6)]
        mp = [c[:bp][:, None, :] for c in chunks]
        ms = [jnp.repeat(c[bp:], ts, axis=0)[None] for c in chunks]
        lru_w = (lru_conv_w[l], row2(lru_conv_b)[l], wa_bd[l], row2(lru_ba)[l], wx_bd[l],
                 row2(lru_bx)[l], row2(lru_lambda)[l])
        lp = diff_lambda[l][None]
        gsub = row2(diff_subln_g)[l]
        wl_, wd_, ws_ = wb[l, :d_rnn], wb[l, d_rnn:d_rnn + w_diff], wb[l, d_rnn + w_diff:]

        z = _in_proj(xp, row2(g_pre_mix)[l], mp[1], mp[0], w_in_p[l], rope_p,
                     tm=tm_in, tiles_per_mod=tp // tm_in, rope_tiles=tp // tm_in)
        o_lru, h_last, conv_new = _lru(z, zeros_conv, zeros_h, *lru_w, nb=bp, t=tp, tt=tt_p)
        od = _diff_attn_prompt(z, lp, gsub, nb=bp, t=tp, tq=tq_p, lam_init=lam_init)
        os_ = _dsa_prompt(z, tri, nb=bp, t=tp, tq=tq_p)
        x1 = _merge(o_lru, od, os_, z, xp, mp[2], wl_, wd_, ws_, wo[l], row2(g_post_mix)[l],
                    tm=256, tiles_per_mod=tp // 256)
        f, keep_a, keep_b = _ffn_up(x1, row2(g_pre_ffn)[l], mp[4], mp[3], wu[l], ffn_conv_w[l],
                                    row2(ffn_conv_b)[l], zeros_ffn, tm=tm_p, tn=256, seq=tp,
                                    tiles_per_mod=tp // tm_p)
        xp = _ffn_down(f, wd[l], x1, mp[5], row2(g_post_ffn)[l], tm=256, tiles_per_mod=tp // 256)
        keep_u = jnp.concatenate([keep_a, keep_b], axis=1).reshape(bp, tp // tm_p, SUBLANES, dff2)
        outs_p.append((z[:, COL_KD:COL_KD + w_diff].reshape(bp, tp, H_DIFF, 2 * DK_DIFF),
                       z[:, COL_VD:COL_VD + w_diff].reshape(bp, tp, H_DIFF, DV_DIFF),
                       z[:, COL_KS:COL_KS + w_dsa].reshape(bp, tp, H_DSA, DH_DSA),
                       z[:, COL_VS:COL_VS + w_dsa].reshape(bp, tp, H_DSA, DH_DSA),
                       z[:, COL_SM:COL_SM + D_IDX].reshape(bp, tp, D_IDX),
                       h_last[:, 0], conv_new,
                       keep_u[:, -1, SUBLANES - (FFN_CONV_W - 1):]))

        z = _in_proj(xs, row2(g_pre_mix)[l], ms[1], ms[0], w_in_p[l], rope_s,
                     tm=ns_, tiles_per_mod=1, rope_tiles=1)
        o_lru, h_last, conv_new = _lru(z, state_lru_conv[l], state_lru_h[l][:, None, :], *lru_w,
                                       nb=bs, t=ts, tt=ts)
        kn_i = _pad_rows(z[:, COL_SM:COL_SM + D_IDX], bs, ts)
        sel = _sample_select(z, cik, page_table, tri, kn_i, layer=l, nb=bs, t=ts,
                             pp=min(pp_idx, page_table.shape[1]))
        os_ = _paged_attn(z, csk, csv, page_table, _pad_rows(z[:, COL_KS:COL_KS + w_dsa], bs, ts),
                          _pad_rows(z[:, COL_VS:COL_VS + w_dsa], bs, ts), (sel,), layer=l, nb=bs,
                          t=ts, pp=min(pp_kv, page_table.shape[1]), mode="dsa", q_col=COL_QS)
        kn_d = z[:, COL_KD:COL_KD + w_diff].reshape(ns_ * H_DIFF, 2 * DK_DIFF)
        vn_d = z[:, COL_VD:COL_VD + w_diff].reshape(ns_ * H_DIFF, DV_DIFF)
        od = _paged_attn(z, cdk, cdv, page_table, _pad_rows(kn_d, bs, ts * H_DIFF),
                         _pad_rows(vn_d, bs, ts * H_DIFF), (lp, gsub), layer=l,
                         nb=bs, t=ts, pp=min(pp_kv, page_table.shape[1]), mode="diff", q_col=COL_QD, lam_init=lam_init)
        x1 = _merge(o_lru, od, os_, z, xs, ms[2], wl_, wd_, ws_, wo[l], row2(g_post_mix)[l],
                    tm=ns_, tiles_per_mod=1)
        stf = state_ffn_conv[l]
        st_rows = jnp.stack([
            jnp.pad(stf[:, jj:, :], ((0, 0), (0, ts - (FFN_CONV_W - 1 - jj)), (0, 0))).reshape(ns_, dff2)
            for jj in range(FFN_CONV_W - 1)])
        f, keep_a, keep_b = _ffn_up(x1, row2(g_pre_ffn)[l], ms[4], ms[3], wu[l], ffn_conv_w[l],
                                    row2(ffn_conv_b)[l], st_rows, tm=ns_, tn=256, seq=ts,
                                    tiles_per_mod=1)
        xs = _ffn_down(f, wd[l], x1, ms[5], row2(g_post_ffn)[l], tm=ns_, tiles_per_mod=1)
        u_all = jnp.concatenate([keep_a, keep_b], axis=1).reshape(bs, ts, dff2)
        outs_s.append((z[:, COL_KD:COL_KD + w_diff].reshape(bs, ts, H_DIFF, 2 * DK_DIFF),
                       z[:, COL_VD:COL_VD + w_diff].reshape(bs, ts, H_DIFF, DV_DIFF),
                       z[:, COL_KS:COL_KS + w_dsa].reshape(bs, ts, H_DSA, DH_DSA),
                       z[:, COL_VS:COL_VS + w_dsa].reshape(bs, ts, H_DSA, DH_DSA),
                       z[:, COL_SM:COL_SM + D_IDX].reshape(bs, ts, D_IDX),
                       h_last[:, 0], conv_new,
                       u_all[:, ts - (FFN_CONV_W - 1):]))

    stack = lambda rows: [jnp.stack([r[k] for r in rows]) for k in range(8)]
    return (xp.reshape(bp, tp, d), xs.reshape(bs, ts, d), *stack(outs_p), *stack(outs_s))
```

```python
import functools
import math

import jax
import jax.numpy as jnp
from jax import lax
from jax.experimental import pallas as pl
from jax.experimental.pallas import tpu as pltpu

F32 = jnp.float32
MXU_DTYPE = jnp.bfloat16

N_RNN_BLOCKS = 16
CONV_W = 4
LRU_C = 8.0
H_DIFF, DK_DIFF = 4, 64
DV_DIFF = 2 * DK_DIFF
H_DSA, DH_DSA = 8, 64
H_IDX, D_IDX = 8, 64
TOPK_MAX = 256
ROPE_THETA = 500000.0
FFN_CONV_W = 3
EPS = 1e-6
NEG_BIG = -1e30

LANES = 128
SUBLANES = 8
MXU_TILE = 256
MIB = 1024 * 1024
CAUSAL_SPAN = 512

ZT = 512
COL_XL, COL_YL = 0, 1024
COL_QD, COL_KD, COL_VD = 2048, 2560, 3072
COL_QS, COL_KS, COL_VS = 3584, 4096, 4608
COL_QI, COL_SM, COL_G = 5120, 5632, 6144
ZW = 9216
ROPE_FULL_TILES = tuple(c // ZT for c in (COL_QD, COL_KD, COL_QS, COL_KS, COL_QI))
ROPE_HEAD_TILE = COL_SM // ZT


def _cparams(n_axes, vmem_mib):
    return pltpu.CompilerParams(dimension_semantics=("arbitrary",) * n_axes,
                                vmem_limit_bytes=vmem_mib * MIB)


def _sigmoid(x):
    return 1.0 / (1.0 + jnp.exp(-x))


def _gelu_tanh(x):
    return 0.5 * x * (1.0 + jnp.tanh(math.sqrt(2.0 / math.pi) * (x + 0.044715 * (x * x * x))))


def _dot(a, b):
    return jnp.dot(a, b, preferred_element_type=F32)


def _dot_nt(a, b):
    return lax.dot_general(a, b, (((1,), (1,)), ((), ())), preferred_element_type=F32)


def _rms(x, g):
    return x * lax.rsqrt(jnp.mean(x * x, axis=-1, keepdims=True) + EPS) * g


def _ada_kernel(c_ref, w_ref, b_ref, o_ref):
    c = c_ref[...]
    a = (c * _sigmoid(c)).astype(MXU_DTYPE)
    o_ref[...] = _dot(a, w_ref[...].astype(MXU_DTYPE)) + b_ref[...]


def _ada(c_all, w_ada, b_ada):
    depth, d, w6 = w_ada.shape
    rows = c_all.shape[0]
    tn = 1536
    return pl.pallas_call(
        _ada_kernel,
        grid=(depth, w6 // tn),
        in_specs=[pl.BlockSpec((rows, d), lambda l, j: (0, 0)),
                  pl.BlockSpec((None, d, tn), lambda l, j: (l, 0, j)),
                  pl.BlockSpec((None, 1, tn), lambda l, j: (l, 0, j))],
        out_specs=pl.BlockSpec((None, rows, tn), lambda l, j: (l, 0, j)),
        out_shape=jax.ShapeDtypeStruct((depth, rows, w6), F32),
        compiler_params=_cparams(2, 40),
        name="ada_mod",
    )(c_all, w_ada, b_ada.reshape(depth, 1, w6))


def _in_proj_kernel(x_ref, g_ref, sc_ref, sh_ref, w_ref, rc_ref, rs1_ref, rs2_ref, z_ref, h_ref):
    j = pl.program_id(1)

    @pl.when(j == 0)
    def _():
        y = _rms(x_ref[...], g_ref[...])
        h_ref[...] = (y * (1.0 + sc_ref[0]) + sh_ref[0]).astype(h_ref.dtype)

    acc = _dot(h_ref[...], w_ref[:, pl.ds(pl.multiple_of(j * ZT, ZT), ZT)])

    def roped(n_chunks):
        outs = []
        for c in range(ZT // LANES):
            ch = acc[:, LANES * c:LANES * (c + 1)]
            if c < n_chunks:
                ch = (ch * rc_ref[...] + pltpu.roll(ch, LANES - 8, 1) * rs1_ref[...]
                      + pltpu.roll(ch, 8, 1) * rs2_ref[...])
            outs.append(ch)
        return jnp.concatenate(outs, axis=1)

    is_full = j == ROPE_FULL_TILES[0]
    for t in ROPE_FULL_TILES[1:]:
        is_full = is_full | (j == t)
    is_head = j == ROPE_HEAD_TILE

    @pl.when(is_full)
    def _():
        z_ref[...] = roped(ZT // LANES)

    @pl.when(is_head)
    def _():
        z_ref[...] = roped(1)

    @pl.when(jnp.logical_not(is_full | is_head))
    def _():
        z_ref[...] = acc


def _in_proj(x, g, sc, sh, w, rope, *, tm, tiles_per_mod, rope_tiles):
    n, d = x.shape
    r = sc.shape[1]
    rc, rs1, rs2 = rope
    mod_spec = pl.BlockSpec((1, r, d), lambda i, j: (i // tiles_per_mod, 0, 0))
    rope_spec = pl.BlockSpec((tm, LANES), lambda i, j: (i % rope_tiles, 0))
    return pl.pallas_call(
        _in_proj_kernel,
        grid=(n // tm, ZW // ZT),
        in_specs=[pl.BlockSpec((tm, d), lambda i, j: (i, 0)),
                  pl.BlockSpec((1, d), lambda i, j: (0, 0)),
                  mod_spec, mod_spec,
                  pl.BlockSpec((d, ZW), lambda i, j: (0, 0), pipeline_mode=pl.Buffered(1)),
                  rope_spec, rope_spec, rope_spec],
        out_specs=pl.BlockSpec((tm, ZT), lambda i, j: (i, j)),
        out_shape=jax.ShapeDtypeStruct((n, ZW), F32),
        scratch_shapes=[pltpu.VMEM((tm, d), MXU_DTYPE)],
        compiler_params=_cparams(2, 48),
        name="in_proj",
    )(x, g, sc, sh, w, rc, rs1, rs2)


def _lru_kernel(xl_ref, yl_ref, stc_ref, sth_ref, cw_ref, cb_ref, wa_ref, ba_ref, wx_ref, bx_ref,
                lam_ref, o_ref, hlast_ref, convout_ref, xbuf, hcar, *, tt):
    ti = pl.program_id(1)
    w1 = CONV_W - 1
    c = xl_ref.shape[1]

    @pl.when(ti == 0)
    def _():
        xbuf[SUBLANES - w1:SUBLANES, :] = stc_ref[0]
        hcar[...] = sth_ref[0]

    @pl.when(ti > 0)
    def _():
        xbuf[0:SUBLANES, :] = xbuf[tt:tt + SUBLANES, :]

    x = xl_ref[...]
    xbuf[SUBLANES:SUBLANES + tt, :] = x
    xc = cb_ref[...] + cw_ref[w1:w1 + 1, :] * x
    for jj in range(w1):
        k = w1 - jj
        xc = xc + cw_ref[jj:jj + 1, :] * xbuf[SUBLANES - k:SUBLANES - k + tt, :]

    xcb = xc.astype(MXU_DTYPE)
    ra, ri = [], []
    for q in range(c // MXU_TILE):
        blk = xcb[:, MXU_TILE * q:MXU_TILE * (q + 1)]
        ra.append(_dot(blk, wa_ref[q]))
        ri.append(_dot(blk, wx_ref[q]))
    r = _sigmoid(jnp.concatenate(ra, axis=1) + ba_ref[...])
    i = _sigmoid(jnp.concatenate(ri, axis=1) + bx_ref[...])
    nl = -lam_ref[...]
    softplus = jnp.maximum(nl, 0.0) + jnp.log(1.0 + jnp.exp(-jnp.abs(nl)))
    log_a = (-LRU_C) * r * softplus
    a = jnp.exp(log_a)
    u = jnp.sqrt(1.0 - jnp.exp(2.0 * log_a)) * i * xc

    row = lax.broadcasted_iota(jnp.int32, (tt, c), 0)
    s = 1
    while s < tt:
        if s < SUBLANES:
            a_sh = jnp.where(row >= s, pltpu.roll(a, s, 0), 1.0)
            u_sh = jnp.where(row >= s, pltpu.roll(u, s, 0), 0.0)
        else:
            a_sh = jnp.concatenate([jnp.ones((s, c), F32), a[:tt - s]], axis=0)
            u_sh = jnp.concatenate([jnp.zeros((s, c), F32), u[:tt - s]], axis=0)
        u = u + a * u_sh
        a = a * a_sh
        s *= 2
    h = a * hcar[...] + u
    hcar[...] = h[tt - 1:tt]
    hlast_ref[0] = h[tt - 1:tt]
    convout_ref[0] = xbuf[SUBLANES + tt - w1:SUBLANES + tt, :]
    o_ref[...] = (h * _gelu_tanh(yl_ref[...])).astype(o_ref.dtype)


def _lru(z, st_conv, st_h, cw, cb, wa_bd, ba, wx_bd, bx, lam, *, nb, t, tt):
    c = cw.shape[1]
    nt = t // tt
    const2 = lambda b, i: (0, 0)
    const3 = lambda b, i: (0, 0, 0)
    return pl.pallas_call(
        functools.partial(_lru_kernel, tt=tt),
        grid=(nb, nt),
        in_specs=[pl.BlockSpec((tt, c), lambda b, i: (b * nt + i, COL_XL // c)),
                  pl.BlockSpec((tt, c), lambda b, i: (b * nt + i, COL_YL // c)),
                  pl.BlockSpec((1, CONV_W - 1, c), lambda b, i: (b, 0, 0)),
                  pl.BlockSpec((1, 1, c), lambda b, i: (b, 0, 0)),
                  pl.BlockSpec((CONV_W, c), const2),
                  pl.BlockSpec((1, c), const2),
                  pl.BlockSpec(wa_bd.shape, const3),
                  pl.BlockSpec((1, c), const2),
                  pl.BlockSpec(wx_bd.shape, const3),
                  pl.BlockSpec((1, c), const2),
                  pl.BlockSpec((1, c), const2)],
        out_specs=[pl.BlockSpec((tt, c), lambda b, i: (b * nt + i, 0)),
                   pl.BlockSpec((1, 1, c), lambda b, i: (b, 0, 0)),
                   pl.BlockSpec((1, CONV_W - 1, c), lambda b, i: (b, 0, 0))],
        out_shape=[jax.ShapeDtypeStruct((nb * t, c), MXU_DTYPE),
                   jax.ShapeDtypeStruct((nb, 1, c), F32),
                   jax.ShapeDtypeStruct((nb, CONV_W - 1, c), F32)],
        scratch_shapes=[pltpu.VMEM((tt + SUBLANES, c), F32), pltpu.VMEM((1, c), F32)],
        compiler_params=_cparams(2, 48),
        name="rg_lru",
    )(z, z, st_conv, st_h, cw, cb, wa_bd, ba, wx_bd, bx, lam)


def _diff_lambda(lp, lam_init):
    a = jnp.sum(lp[0:1] * lp[1:2], axis=1, keepdims=True)
    b = jnp.sum(lp[2:3] * lp[3:4], axis=1, keepdims=True)
    return jnp.exp(a) - jnp.exp(b) + lam_init


def _online_update(carry, s, pv):
    m, l, acc = carry
    m_new = jnp.maximum(m, jnp.max(s, axis=1, keepdims=True))
    alpha = jnp.exp(m - m_new)
    p = jnp.exp(s - m_new)
    l = alpha * l + jnp.sum(p, axis=1, keepdims=True)
    acc = alpha * acc + pv(p.astype(MXU_DTYPE))
    return m_new, l, acc


def _sort_key(score):
    bits = pltpu.bitcast(score, jnp.int32)
    return jnp.where(bits < 0, bits ^ jnp.int32(0x7FFFFFFF), bits)


def _topk_select(key, n_sel, tri):
    rows, length = key.shape
    int_min = jnp.int32(-2 ** 31)

    def count_ge(t):
        return jnp.sum(jnp.where(key >= t, 1.0, 0.0), axis=1, keepdims=True)

    ans = jnp.where(count_ge(jnp.zeros((rows, 1), jnp.int32)) >= n_sel, jnp.int32(0), int_min)

    def body(it, ans):
        t = ans | lax.shift_left(jnp.int32(1), jnp.int32(30) - it)
        return jnp.where(count_ge(t) >= n_sel, t, ans)

    thr = lax.fori_loop(0, 31, body, ans)
    gt = jnp.where(key > thr, 1.0, 0.0)
    eq = jnp.where(key == thr, 1.0, 0.0)
    need = n_sel - jnp.sum(gt, axis=1, keepdims=True)
    run = jnp.zeros((rows, 1), F32)
    out = []
    for c in range(length // LANES):
        eqc = eq[:, LANES * c:LANES * (c + 1)]
        before = _dot(eqc.astype(MXU_DTYPE), tri) + run
        out.append(gt[:, LANES * c:LANES * (c + 1)] + eqc * jnp.where(before < need, 1.0, 0.0))
        run = run + jnp.sum(eqc, axis=1, keepdims=True)
    return jnp.concatenate(out, axis=1)


def _diff_attn_kernel(q_ref, k_ref, v_ref, lp_ref, g_ref, o_ref, kb, vb, s_s, *, tq, lam_init):
    qi = pl.program_id(2)

    @pl.when(qi == 0)
    def _():
        kb[...] = k_ref[...].astype(MXU_DTYPE)
        vb[...] = v_ref[...].astype(MXU_DTYPE)

    q = q_ref[...] * (DK_DIFF ** -0.5)
    lane = lax.broadcasted_iota(jnp.int32, q.shape, 1)
    qq = jnp.concatenate([jnp.where(lane < DK_DIFF, q, 0.0), jnp.where(lane >= DK_DIFF, q, 0.0)],
                         axis=0).astype(MXU_DTYPE)

    lam = _diff_lambda(lp_ref[0], lam_init)

    span = max(CAUSAL_SPAN, tq)

    def attend(length):
        chunks = [(c, c + span) for c in range(0, length, span)]
        m = jnp.full((2 * tq, 1), NEG_BIG, F32)
        for lo, hi in chunks:
            s = _dot_nt(qq, kb[lo:hi, :])
            if hi == length:
                r = lax.broadcasted_iota(jnp.int32, s.shape, 0)
                qpos = qi * tq + jnp.where(r >= tq, r - tq, r)
                s = jnp.where(lo + lax.broadcasted_iota(jnp.int32, s.shape, 1) <= qpos, s, NEG_BIG)
            s_s[:, lo:hi] = s
            m = jnp.maximum(m, jnp.max(s, axis=1, keepdims=True))
        l = jnp.zeros((2 * tq, 1), F32)
        acc = jnp.zeros((2 * tq, DV_DIFF), F32)
        for lo, hi in chunks:
            p = jnp.exp(s_s[:, lo:hi] - m)
            l = l + jnp.sum(p, axis=1, keepdims=True)
            acc = acc + _dot(p.astype(MXU_DTYPE), vb[lo:hi, :])
        on = acc / l
        o = on[:tq] - lam * on[tq:]
        o_ref[...] = (_rms(o, g_ref[...]) * (1.0 - lam_init)).astype(o_ref.dtype)

    for v in range(kb.shape[0] // span):
        pl.when((qi * tq) // span == v)(functools.partial(attend, (v + 1) * span))


def _diff_attn_prompt(z, lp, g, *, nb, t, tq, lam_init):
    nq = t // tq
    hw = 2 * DK_DIFF
    return pl.pallas_call(
        functools.partial(_diff_attn_kernel, tq=tq, lam_init=lam_init),
        grid=(nb, H_DIFF, nq),
        in_specs=[pl.BlockSpec((tq, hw), lambda b, h, i: (b * nq + i, COL_QD // hw + h)),
                  pl.BlockSpec((t, hw), lambda b, h, i: (b, COL_KD // hw + h)),
                  pl.BlockSpec((t, DV_DIFF), lambda b, h, i: (b, COL_VD // DV_DIFF + h)),
                  pl.BlockSpec((1, 4, DK_DIFF), lambda b, h, i: (0, 0, 0)),
                  pl.BlockSpec((1, DV_DIFF), lambda b, h, i: (0, 0))],
        out_specs=pl.BlockSpec((tq, DV_DIFF), lambda b, h, i: (b * nq + i, h)),
        out_shape=jax.ShapeDtypeStruct((nb * t, H_DIFF * DV_DIFF), MXU_DTYPE),
        scratch_shapes=[pltpu.VMEM((t, hw), MXU_DTYPE), pltpu.VMEM((t, DV_DIFF), MXU_DTYPE),
                        pltpu.VMEM((2 * tq, t), F32)],
        compiler_params=_cparams(3, 40),
        name="diff_attn_prompt",
    )(z, z, z, lp, g)


def _dsa_kernel(qs_ref, ks_ref, vs_ref, qi_ref, ki_ref, wi_ref, tri_ref, o_ref, ksb, vsb, kib,
                *, tq, n_sel):
    qt = pl.program_id(1)

    @pl.when(qt == 0)
    def _():
        ksb[...] = ks_ref[...].astype(MXU_DTYPE)
        vsb[...] = vs_ref[...].astype(MXU_DTYPE)
        kib[...] = ki_ref[...].astype(MXU_DTYPE)

    def select_and_attend(length):
        lane = lax.broadcasted_iota(jnp.int32, (tq, LANES), 1)
        halves = (lane < D_IDX, lane >= D_IDX)

        w = wi_ref[...] * (H_IDX ** -0.5 * D_IDX ** -0.5)
        ki = kib[0:length, :]
        score = jnp.zeros((tq, length), F32)
        for m in range(H_IDX // 2):
            ch = qi_ref[:, LANES * m:LANES * (m + 1)]
            for half in range(2):
                d = _dot_nt(jnp.where(halves[half], ch, 0.0).astype(MXU_DTYPE), ki)
                hh = 2 * m + half
                score = score + w[:, hh:hh + 1] * jnp.maximum(d, 0.0)

        col = lax.broadcasted_iota(jnp.int32, (tq, length), 1)
        qpos = qt * tq + lax.broadcasted_iota(jnp.int32, (tq, length), 0)
        valid = col <= qpos
        key = _sort_key(jnp.where(valid, score, -jnp.inf))
        sel = _topk_select(key, n_sel, tri_ref[...]) * jnp.where(valid, 1.0, 0.0)
        keep = sel > 0.5

        outs = []
        for m in range(H_DSA // 2):
            qc = qs_ref[:, LANES * m:LANES * (m + 1)] * (DH_DSA ** -0.5)
            kc = ksb[0:length, LANES * m:LANES * (m + 1)]
            vc = vsb[0:length, LANES * m:LANES * (m + 1)]
            pair = []
            for half in range(2):
                s = _dot_nt(jnp.where(halves[half], qc, 0.0).astype(MXU_DTYPE), kc)
                s = jnp.where(keep, s, NEG_BIG)
                p = jnp.exp(s - jnp.max(s, axis=1, keepdims=True))
                l = jnp.sum(p, axis=1, keepdims=True)
                pair.append(_dot(p.astype(MXU_DTYPE), vc) / l)
            outs.append(jnp.where(halves[0], pair[0], pair[1]))
        o_ref[...] = jnp.concatenate(outs, axis=1).astype(o_ref.dtype)

    span = max(CAUSAL_SPAN, tq)
    for v in range(ks_ref.shape[0] // span):
        pl.when((qt * tq) // span == v)(functools.partial(select_and_attend, (v + 1) * span))


def _dsa_prompt(z, tri, *, nb, t, tq):
    nq = t // tq
    w = H_DSA * DH_DSA
    n_sel = min(TOPK_MAX, t // 4)
    assert max(CAUSAL_SPAN, tq) >= n_sel and t % max(CAUSAL_SPAN, tq) == 0
    return pl.pallas_call(
        functools.partial(_dsa_kernel, tq=tq, n_sel=n_sel),
        grid=(nb, nq),
        in_specs=[pl.BlockSpec((tq, w), lambda b, i: (b * nq + i, COL_QS // w)),
                  pl.BlockSpec((t, w), lambda b, i: (b, COL_KS // w)),
                  pl.BlockSpec((t, w), lambda b, i: (b, COL_VS // w)),
                  pl.BlockSpec((tq, w), lambda b, i: (b * nq + i, COL_QI // w)),
                  pl.BlockSpec((t, LANES), lambda b, i: (b, COL_SM // LANES)),
                  pl.BlockSpec((tq, LANES), lambda b, i: (b * nq + i, COL_SM // LANES + 1)),
                  pl.BlockSpec((LANES, LANES), lambda b, i: (0, 0))],
        out_specs=pl.BlockSpec((tq, w), lambda b, i: (b * nq + i, 0)),
        out_shape=jax.ShapeDtypeStruct((nb * t, w), MXU_DTYPE),
        scratch_shapes=[pltpu.VMEM((t, w), MXU_DTYPE), pltpu.VMEM((t, w), MXU_DTYPE),
                        pltpu.VMEM((t, LANES), MXU_DTYPE)],
        compiler_params=_cparams(2, 56),
        name="dsa_prompt",
    )(z, z, z, z, z, z, tri)


def _sample_score_kernel(pt_ref, qi_ref, wi_ref, kn_ref, *rest, pp, n_pages, t):
    pages = rest[:pp]
    score, qrows, wcol = rest[pp:]
    p = pl.program_id(1)
    rows = H_IDX * t

    @pl.when(p == 0)
    def _():
        q = qi_ref[...]
        qrows[...] = jnp.concatenate([q[:, D_IDX * h:D_IDX * (h + 1)] for h in range(H_IDX)],
                                     axis=0).astype(MXU_DTYPE)
        w = wi_ref[...] * (H_IDX ** -0.5 * D_IDX ** -0.5)
        wcol[...] = jnp.concatenate([w[:, h:h + 1] for h in range(H_IDX)], axis=0)

    def page_scores(d):
        d = wcol[...] * jnp.maximum(d, 0.0)
        tot = d[0:t]
        for h in range(1, H_IDX):
            tot = tot + d[h * t:(h + 1) * t]
        return tot

    for r in range(pp):
        off = pl.multiple_of((p * pp + r) * LANES, LANES)
        score[:, pl.ds(off, LANES)] = page_scores(_dot(qrows[...], pages[r][...].astype(MXU_DTYPE)))

    @pl.when(p == n_pages // pp - 1)
    def _():
        new = page_scores(_dot_nt(qrows[...], kn_ref[...].astype(MXU_DTYPE)))
        ci = lax.broadcasted_iota(jnp.int32, (t, LANES), 1)
        ri = lax.broadcasted_iota(jnp.int32, (t, LANES), 0)
        score[:, n_pages * LANES:] = jnp.where(ci <= ri, new, -jnp.inf)


def _select_rows_kernel(score_ref, tri_ref, sel_ref, *, n_sel):
    sel_ref[...] = _topk_select(_sort_key(score_ref[...]), n_sel, tri_ref[...])


def _sample_select(z, cache_idx, page_table, tri, kn_pad, *, layer, nb, t, pp):
    n_pages = page_table.shape[1]
    page = cache_idx.shape[3]
    assert page == LANES
    lpad = n_pages * page + LANES
    n_sel = min(TOPK_MAX, (n_pages * page + t) // 4)
    w = H_IDX * D_IDX
    page_specs = [pl.BlockSpec((None, None, D_IDX, page),
                               functools.partial(lambda b, p, pt, r: (layer, pt[b, p * pp + r], 0, 0), r=r))
                  for r in range(pp)]
    grid_spec = pltpu.PrefetchScalarGridSpec(
        num_scalar_prefetch=1,
        grid=(nb, n_pages // pp),
        in_specs=[pl.BlockSpec((t, w), lambda b, p, pt: (b, COL_QI // w)),
                  pl.BlockSpec((t, LANES), lambda b, p, pt: (b, COL_SM // LANES + 1)),
                  pl.BlockSpec((None, LANES, D_IDX), lambda b, p, pt: (b, 0, 0))] + page_specs,
        out_specs=pl.BlockSpec((t, lpad), lambda b, p, pt: (b, 0)),
        scratch_shapes=[pltpu.VMEM((H_IDX * t, D_IDX), MXU_DTYPE), pltpu.VMEM((H_IDX * t, 1), F32)])
    score = pl.pallas_call(
        functools.partial(_sample_score_kernel, pp=pp, n_pages=n_pages, t=t),
        grid_spec=grid_spec,
        out_shape=jax.ShapeDtypeStruct((nb * t, lpad), F32),
        compiler_params=_cparams(2, 32),
        name="sample_score",
    )(page_table, z, z, kn_pad, *([cache_idx] * pp))
    tr = math.gcd(nb * t, 64)
    sel = pl.pallas_call(
        functools.partial(_select_rows_kernel, n_sel=n_sel),
        grid=(nb * t // tr,),
        in_specs=[pl.BlockSpec((tr, lpad), lambda i: (i, 0)),
                  pl.BlockSpec((LANES, LANES), lambda i: (0, 0))],
        out_specs=pl.BlockSpec((tr, lpad), lambda i: (i, 0)),
        out_shape=jax.ShapeDtypeStruct((nb * t, lpad), F32),
        compiler_params=_cparams(1, 48),
        name="sample_select",
    )(score, tri)
    return sel.reshape(nb, t, lpad)


def _paged_attn_kernel(pt_ref, q_ref, kn_ref, vn_ref, *rest, pp, n_pages, t, mode, lam_init):
    if mode == "dsa":
        sel_ref, rest = rest[0], rest[1:]
    else:
        lp_ref, g_ref, rest = rest[0], rest[1], rest[2:]
    kpages, vpages = rest[:pp], rest[pp:2 * pp]
    o_ref, qrows, m_s, l_s, acc_s = rest[2 * pp:]
    p = pl.program_id(1)
    rows, qw = qrows.shape
    groups = rows // t

    @pl.when(p == 0)
    def _():
        if mode == "dsa":
            q = jnp.concatenate([q_ref[...] * (DH_DSA ** -0.5)] * groups, axis=0)
            grp = lax.broadcasted_iota(jnp.int32, (rows, qw), 0) // t
            lane_grp = lax.broadcasted_iota(jnp.int32, (rows, qw), 1) // DH_DSA
            qrows[...] = jnp.where(lane_grp == grp, q, 0.0).astype(MXU_DTYPE)
        else:
            q = q_ref[...] * (DK_DIFF ** -0.5)
            lane = lax.broadcasted_iota(jnp.int32, (t, qw), 1)
            parts = []
            for c in range(2):
                keep = (lane < DK_DIFF) if c == 0 else (lane >= DK_DIFF)
                for h in range(H_DIFF):
                    parts.append(jnp.where(keep, q[:, qw * h:qw * (h + 1)], 0.0))
            qrows[...] = jnp.concatenate(parts, axis=0).astype(MXU_DTYPE)
        m_s[...] = jnp.full(m_s.shape, NEG_BIG, F32)
        l_s[...] = jnp.zeros(l_s.shape, F32)
        acc_s[...] = jnp.zeros(acc_s.shape, F32)

    def update(s, pv):
        m, l, acc = _online_update((m_s[...], l_s[...], acc_s[...]), s, pv)
        m_s[...] = m
        l_s[...] = l
        acc_s[...] = acc

    def row_head(shape):
        return (lax.broadcasted_iota(jnp.int32, shape, 0) // t) % H_DIFF

    if mode == "dsa":
        kt = jnp.concatenate([r[...] for r in kpages], axis=1).astype(MXU_DTYPE)
        vt = jnp.concatenate([r[...] for r in vpages], axis=1).astype(MXU_DTYPE)
        s = _dot(qrows[...], kt)
        off = pl.multiple_of(p * (pp * LANES), pp * LANES)
        selc = sel_ref[:, pl.ds(off, pp * LANES)]
        s = jnp.where(jnp.concatenate([selc] * groups, axis=0) > 0.5, s, NEG_BIG)
        update(s, lambda pr: _dot_nt(pr, vt))
    else:
        ks = jnp.concatenate([r[...] for r in kpages], axis=0).astype(MXU_DTYPE)
        vs = jnp.concatenate([r[...] for r in vpages], axis=0).astype(MXU_DTYPE)
        s = _dot_nt(qrows[...], ks)
        col_head = lax.broadcasted_iota(jnp.int32, s.shape, 1) % H_DIFF
        s = jnp.where(col_head == row_head(s.shape), s, NEG_BIG)
        update(s, lambda pr: _dot(pr, vs))

    @pl.when(p == n_pages // pp - 1)
    def _():
        kn = kn_ref[...].astype(MXU_DTYPE)
        vn = vn_ref[...].astype(MXU_DTYPE)
        s = _dot_nt(qrows[...], kn)
        ci = lax.broadcasted_iota(jnp.int32, s.shape, 1)
        ti = lax.broadcasted_iota(jnp.int32, s.shape, 0) % t
        if mode == "dsa":
            seln = sel_ref[:, n_pages * LANES:]
            keep = (ci <= ti) & (jnp.concatenate([seln] * groups, axis=0) > 0.5)
        else:
            keep = (ci // H_DIFF <= ti) & (ci % H_DIFF == row_head(s.shape))
        update(jnp.where(keep, s, NEG_BIG), lambda pr: _dot(pr, vn))
        on = acc_s[...] / l_s[...]
        if mode == "dsa":
            lane = lax.broadcasted_iota(jnp.int32, (t, qw), 1)
            out = on[0:t]
            for h in range(1, H_DSA):
                out = jnp.where(lane // DH_DSA == h, on[h * t:(h + 1) * t], out)
            o_ref[...] = out.astype(o_ref.dtype)
        else:
            half = H_DIFF * t
            o = on[:half] - _diff_lambda(lp_ref[0], lam_init) * on[half:]
            o = _rms(o, g_ref[...]) * (1.0 - lam_init)
            o_ref[...] = jnp.concatenate([o[h * t:(h + 1) * t] for h in range(H_DIFF)],
                                         axis=1).astype(o_ref.dtype)


def _paged_attn(z, cache_k, cache_v, page_table, kn_pad, vn_pad, extra, *, layer, nb, t, pp, mode,
                q_col, lam_init=0.0):
    n_pages = page_table.shape[1]
    prow, pcol = cache_k.shape[2], cache_k.shape[3]
    if mode == "dsa":
        rows, qw, width, accw = H_DSA * t, H_DSA * DH_DSA, H_DSA * DH_DSA, H_DSA * DH_DSA
    else:
        rows, qw, width, accw = 2 * H_DIFF * t, 2 * DK_DIFF, H_DIFF * DV_DIFF, DV_DIFF

    def page_spec(r):
        return pl.BlockSpec((None, None, prow, pcol),
                            functools.partial(lambda b, p, pt, r: (layer, pt[b, p * pp + r], 0, 0), r=r))

    new_spec = pl.BlockSpec((None, LANES, kn_pad.shape[2]), lambda b, p, pt: (b, 0, 0))
    in_specs = [pl.BlockSpec((t, width), lambda b, p, pt: (b, q_col // width)), new_spec, new_spec]
    if mode == "dsa":
        (sel,) = extra
        in_specs.append(pl.BlockSpec((None, t, sel.shape[2]), lambda b, p, pt: (b, 0, 0)))
    else:
        in_specs += [pl.BlockSpec((1, 4, DK_DIFF), lambda b, p, pt: (0, 0, 0)),
                     pl.BlockSpec((1, DV_DIFF), lambda b, p, pt: (0, 0))]
    in_specs += [page_spec(r) for r in range(pp)] * 2
    grid_spec = pltpu.PrefetchScalarGridSpec(
        num_scalar_prefetch=1,
        grid=(nb, n_pages // pp),
        in_specs=in_specs,
        out_specs=pl.BlockSpec((t, width), lambda b, p, pt: (b, 0)),
        scratch_shapes=[pltpu.VMEM((rows, qw), MXU_DTYPE), pltpu.VMEM((rows, 1), F32),
                        pltpu.VMEM((rows, 1), F32), pltpu.VMEM((rows, accw), F32)])
    return pl.pallas_call(
        functools.partial(_paged_attn_kernel, pp=pp, n_pages=n_pages, t=t, mode=mode, lam_init=lam_init),
        grid_spec=grid_spec,
        out_shape=jax.ShapeDtypeStruct((nb * t, width), MXU_DTYPE),
        compiler_params=_cparams(2, 48),
        name="paged_attn_" + mode,
    )(page_table, z, kn_pad, vn_pad, *extra, *([cache_k] * pp), *([cache_v] * pp))


def _merge_kernel(ol_ref, od_ref, os_ref, zg_ref, x_ref, ga_ref, wl_ref, wd_ref, ws_ref, wo_ref,
                  g_ref, o_ref):
    d = x_ref.shape[1]
    zg = zg_ref[...]
    merged = (_sigmoid(zg[:, 0:d]) * _dot(ol_ref[...], wl_ref[...])
              + _sigmoid(zg[:, d:2 * d]) * _dot(od_ref[...], wd_ref[...])
              + _sigmoid(zg[:, 2 * d:3 * d]) * _dot(os_ref[...], ws_ref[...]))
    y = _dot(merged.astype(MXU_DTYPE), wo_ref[...])
    o_ref[...] = x_ref[...] + ga_ref[0] * _rms(y, g_ref[...])


def _merge(o_lru, od, os_, z, x, ga, wl, wd, ws, wo, g, *, tm, tiles_per_mod):
    n, d = x.shape
    r = ga.shape[1]
    row = lambda i: (i, 0)
    const = lambda i: (0, 0)
    return pl.pallas_call(
        _merge_kernel,
        grid=(n // tm,),
        in_specs=[pl.BlockSpec((tm, o_lru.shape[1]), row),
                  pl.BlockSpec((tm, od.shape[1]), row),
                  pl.BlockSpec((tm, os_.shape[1]), row),
                  pl.BlockSpec((tm, 3 * d), lambda i: (i, COL_G // (3 * d))),
                  pl.BlockSpec((tm, d), row),
                  pl.BlockSpec((1, r, d), lambda i: (i // tiles_per_mod, 0, 0)),
                  pl.BlockSpec(wl.shape, const), pl.BlockSpec(wd.shape, const),
                  pl.BlockSpec(ws.shape, const), pl.BlockSpec(wo.shape, const),
                  pl.BlockSpec((1, d), const)],
        out_specs=pl.BlockSpec((tm, d), row),
        out_shape=jax.ShapeDtypeStruct((n, d), F32),
        compiler_params=_cparams(1, 48),
        name="merge_out",
    )(o_lru, od, os_, z, x, ga, wl, wd, ws, wo, g)


HALO = 16


def _ffn_up_kernel(x_ref, xh_ref, g_ref, sc_ref, sh_ref, w_ref, cwa_ref, cwb_ref, cba_ref,
                   cbb_ref, sta_ref, stb_ref, f_ref, keepa_ref, keepb_ref, h_s, ua_s, ub_s,
                   *, tm, tn, seq, keep):
    i = pl.program_id(0)
    j = pl.program_id(1)
    w1 = FFN_CONV_W - 1
    long_seq = seq >= tm
    dff = w_ref.shape[1] // 2

    @pl.when(j == 0)
    def _():
        def normed(xv):
            return (_rms(xv, g_ref[...]) * (1.0 + sc_ref[0]) + sh_ref[0]).astype(h_s.dtype)
        if long_seq:
            h_s[0:HALO, :] = normed(xh_ref[...])
        else:
            h_s[0:HALO, :] = jnp.zeros((HALO, h_s.shape[1]), h_s.dtype)
        h_s[HALO:, :] = normed(x_ref[...])

    def half(col0, cw_ref, cb_ref, st_ref, keep_ref, u_s):
        w = w_ref[:, pl.ds(pl.multiple_of(col0 + j * tn, tn), tn)]
        u_s[...] = _dot(h_s[...], w)
        if long_seq:
            first = i % (seq // tm) == 0
            u_s[HALO - w1:HALO, :] = jnp.where(first, st_ref[0], u_s[HALO - w1:HALO, :])
        u = u_s[HALO:, :]
        keep_ref[...] = u_s[HALO + tm - keep:, :]
        y = cb_ref[...] + cw_ref[w1:w1 + 1, :] * u
        if not long_seq:
            tpos = lax.broadcasted_iota(jnp.int32, u.shape, 0) % seq
        for jj in range(w1):
            k = w1 - jj
            prev = u_s[HALO - k:HALO - k + tm, :]
            if not long_seq:
                prev = jnp.where(tpos >= k, prev, st_ref[jj])
            y = y + cw_ref[jj:jj + 1, :] * prev
        return y

    ya = half(0, cwa_ref, cba_ref, sta_ref, keepa_ref, ua_s)
    yb = half(dff, cwb_ref, cbb_ref, stb_ref, keepb_ref, ub_s)
    f_ref[...] = (_gelu_tanh(ya) * yb).astype(f_ref.dtype)


def _ffn_up(x, g, sc, sh, w_up, cw, cb, st, *, tm, tn, seq, tiles_per_mod):
    n, d = x.shape
    dff = w_up.shape[1] // 2
    nj = dff // tn
    r = sc.shape[1]
    long_seq = seq >= tm
    keep = SUBLANES if long_seq else tm
    mod_spec = pl.BlockSpec((1, r, d), lambda i, j: (i // tiles_per_mod, 0, 0))
    if long_seq:
        tps = seq // tm
        sta = pl.BlockSpec((1, FFN_CONV_W - 1, tn), lambda i, j: (i // tps, 0, j))
        stb = pl.BlockSpec((1, FFN_CONV_W - 1, tn), lambda i, j: (i // tps, 0, nj + j))
    else:
        sta = pl.BlockSpec((FFN_CONV_W - 1, tm, tn), lambda i, j: (0, i, j))
        stb = pl.BlockSpec((FFN_CONV_W - 1, tm, tn), lambda i, j: (0, i, nj + j))
    n_keep = (n // tm) * keep
    return pl.pallas_call(
        functools.partial(_ffn_up_kernel, tm=tm, tn=tn, seq=seq, keep=keep),
        grid=(n // tm, nj),
        in_specs=[pl.BlockSpec((tm, d), lambda i, j: (i, 0)),
                  pl.BlockSpec((HALO, d), lambda i, j: (jnp.maximum(i * (tm // HALO) - 1, 0), 0)),
                  pl.BlockSpec((1, d), lambda i, j: (0, 0)),
                  mod_spec, mod_spec,
                  pl.BlockSpec((d, 2 * dff), lambda i, j: (0, 0), pipeline_mode=pl.Buffered(1)),
                  pl.BlockSpec((FFN_CONV_W, tn), lambda i, j: (0, j)),
                  pl.BlockSpec((FFN_CONV_W, tn), lambda i, j: (0, nj + j)),
                  pl.BlockSpec((1, tn), lambda i, j: (0, j)),
                  pl.BlockSpec((1, tn), lambda i, j: (0, nj + j)),
                  sta, stb],
        out_specs=[pl.BlockSpec((tm, tn), lambda i, j: (i, j)),
                   pl.BlockSpec((keep, tn), lambda i, j: (i, j)),
                   pl.BlockSpec((keep, tn), lambda i, j: (i, j))],
        out_shape=[jax.ShapeDtypeStruct((n, dff), MXU_DTYPE),
                   jax.ShapeDtypeStruct((n_keep, dff), F32),
                   jax.ShapeDtypeStruct((n_keep, dff), F32)],
        scratch_shapes=[pltpu.VMEM((HALO + tm, d), MXU_DTYPE), pltpu.VMEM((HALO + tm, tn), F32),
                        pltpu.VMEM((HALO + tm, tn), F32)],
        compiler_params=_cparams(2, 48),
        name="ffn_up",
    )(x, x, g, sc, sh, w_up, cw, cw, cb, cb, st, st)


def _ffn_down_kernel(f_ref, w_ref, x_ref, ga_ref, g_ref, o_ref):
    y = _dot(f_ref[...], w_ref[...])
    o_ref[...] = x_ref[...] + ga_ref[0] * _rms(y, g_ref[...])


def _ffn_down(f, w, x, ga, g, *, tm, tiles_per_mod):
    n, d = x.shape
    r = ga.shape[1]
    return pl.pallas_call(
        _ffn_down_kernel,
        grid=(n // tm,),
        in_specs=[pl.BlockSpec((tm, f.shape[1]), lambda i: (i, 0)),
                  pl.BlockSpec(w.shape, lambda i: (0, 0)),
                  pl.BlockSpec((tm, d), lambda i: (i, 0)),
                  pl.BlockSpec((1, r, d), lambda i: (i // tiles_per_mod, 0, 0)),
                  pl.BlockSpec((1, d), lambda i: (0, 0))],
        out_specs=pl.BlockSpec((tm, d), lambda i: (i, 0)),
        out_shape=jax.ShapeDtypeStruct((n, d), F32),
        compiler_params=_cparams(1, 48),
        name="ffn_down",
    )(f, w, x, ga, g)


def _rope_tables(pos):
    rot = D_IDX // 4
    half = rot // 2
    inv = ROPE_THETA ** (-jnp.arange(half, dtype=F32) * (2.0 / rot))
    ang = pos.astype(F32)[:, None] * inv[None, :]
    cos, sin = jnp.cos(ang), jnp.sin(ang)
    p = pos.shape[0]
    one = jnp.ones((p, D_IDX - rot), F32)
    zero = jnp.zeros((p, D_IDX - rot), F32)
    zh = jnp.zeros((p, half), F32)
    c = jnp.concatenate([cos, cos, one], axis=1)
    s1 = jnp.concatenate([-sin, zh, zero], axis=1)
    s2 = jnp.concatenate([zh, sin, zero], axis=1)
    return tuple(jnp.tile(a, (1, LANES // D_IDX)) for a in (c, s1, s2))


def _pack_w_in(w_in):
    depth, d, _ = w_in.shape
    n_main = COL_SM
    ki = w_in[:, :, n_main:n_main + D_IDX]
    wi = w_in[:, :, n_main + D_IDX:n_main + D_IDX + H_IDX]
    gate = w_in[:, :, n_main + D_IDX + H_IDX:]
    pad = jnp.zeros((depth, d, COL_G - COL_SM - 2 * D_IDX - H_IDX), w_in.dtype)
    return jnp.concatenate([w_in[:, :, :n_main], ki, ki, wi, pad, gate], axis=2).astype(MXU_DTYPE)


def _block_diag(w, per):
    depth, nblk, k, _ = w.shape
    w = w.reshape(depth, nblk // per, per, k, k)
    eye = jnp.eye(per, dtype=w.dtype)
    return jnp.einsum("dgpij,pq->dgpiqj", w, eye).reshape(depth, nblk // per, per * k, per * k)


def _pad_rows(a, nb, t):
    a = a.reshape(nb, t, a.shape[1])
    return jnp.pad(a, ((0, 0), (0, LANES - t), (0, 0)))


def kernel(x_prompt, x_sample, cache_diff_k, cache_diff_v, cache_dsa_k, cache_dsa_v, cache_idx_k, state_lru_h, state_lru_conv, state_ffn_conv, page_table, c_prompt, c_sample, w_ada, b_ada, g_pre_mix, g_post_mix, g_pre_ffn, g_post_ffn, w_in, lru_conv_w, lru_conv_b, lru_wa, lru_ba, lru_wx, lru_bx, lru_lambda, diff_lambda, diff_subln_g, w_branch, w_out, w_up, ffn_conv_w, ffn_conv_b, w_down):
    bp, tp, d = x_prompt.shape
    bs, ts, _ = x_sample.shape
    depth = w_in.shape[0]
    d_rnn = lru_conv_w.shape[2]
    dff2 = w_up.shape[2]
    n_pool, page = cache_diff_k.shape[1], cache_diff_k.shape[2]
    past_len = page_table.shape[1] * page
    w_diff, w_dsa = H_DIFF * DV_DIFF, H_DSA * DH_DSA
    np_, ns_ = bp * tp, bs * ts

    w_in_p = _pack_w_in(w_in)
    wa_bd = _block_diag(lru_wa, MXU_TILE // (d_rnn // N_RNN_BLOCKS)).astype(MXU_DTYPE)
    wx_bd = _block_diag(lru_wx, MXU_TILE // (d_rnn // N_RNN_BLOCKS)).astype(MXU_DTYPE)
    wb = w_branch.astype(MXU_DTYPE)
    wo = w_out.astype(MXU_DTYPE)
    wu = w_up.astype(MXU_DTYPE)
    wd = w_down.astype(MXU_DTYPE)
    tri = jnp.triu(jnp.ones((LANES, LANES), F32), k=1).astype(MXU_DTYPE)
    row2 = lambda a: a[:, None, :]

    mod = _ada(jnp.concatenate([c_prompt, c_sample], axis=0), w_ada, b_ada)

    rope_p = _rope_tables(jnp.arange(tp, dtype=jnp.int32))
    pos_s = past_len + jnp.arange(ts, dtype=jnp.int32)
    rope_s = _rope_tables(jnp.tile(pos_s, bs))

    cdk = cache_diff_k.reshape(depth, n_pool, page * H_DIFF, 2 * DK_DIFF)
    cdv = cache_diff_v.reshape(depth, n_pool, page * H_DIFF, DV_DIFF)
    csk = cache_dsa_k.transpose(0, 1, 3, 4, 2).reshape(depth, n_pool, w_dsa, page)
    csv = cache_dsa_v.transpose(0, 1, 3, 4, 2).reshape(depth, n_pool, w_dsa, page)
    cik = cache_idx_k.transpose(0, 1, 3, 2)

    tm_p, tq_p, tt_p = min(1024, tp), 256, 256
    tm_mix = 512
    tm_in = min(1024, tp)
    pp_idx, pp_kv = 32, 16
    xp = x_prompt.reshape(np_, d)
    xs = x_sample.reshape(ns_, d)
    outs_p, outs_s = [], []
    zeros_conv = jnp.zeros((bp, CONV_W - 1, d_rnn), F32)
    zeros_h = jnp.zeros((bp, 1, d_rnn), F32)
    zeros_ffn = jnp.zeros((bp, FFN_CONV_W - 1, dff2), F32)

    for l in range(depth):
        lam_init = 0.8 - 0.6 * math.exp(-0.3 * l)
        m = mod[l]
        chunks = [m[:, k * d:(k + 1) * d] for k in range(6)]
        mp = [c[:bp][:, None, :] for c in chunks]
        ms = [jnp.repeat(c[bp:], ts, axis=0)[None] for c in chunks]
        lru_w = (lru_conv_w[l], row2(lru_conv_b)[l], wa_bd[l], row2(lru_ba)[l], wx_bd[l],
                 row2(lru_bx)[l], row2(lru_lambda)[l])
        lp = diff_lambda[l][None]
        gsub = row2(diff_subln_g)[l]
        wl_, wd_, ws_ = wb[l, :d_rnn], wb[l, d_rnn:d_rnn + w_diff], wb[l, d_rnn + w_diff:]

        z = _in_proj(xp, row2(g_pre_mix)[l], mp[1], mp[0], w_in_p[l], rope_p,
                     tm=tm_in, tiles_per_mod=tp // tm_in, rope_tiles=tp // tm_in)
        o_lru, h_last, conv_new = _lru(z, zeros_conv, zeros_h, *lru_w, nb=bp, t=tp, tt=tt_p)
        od = _diff_attn_prompt(z, lp, gsub, nb=bp, t=tp, tq=tq_p, lam_init=lam_init)
        os_ = _dsa_prompt(z, tri, nb=bp, t=tp, tq=tq_p)
        x1 = _merge(o_lru, od, os_, z, xp, mp[2], wl_, wd_, ws_, wo[l], row2(g_post_mix)[l],
                    tm=tm_mix, tiles_per_mod=tp // tm_mix)
        f, keep_a, keep_b = _ffn_up(x1, row2(g_pre_ffn)[l], mp[4], mp[3], wu[l], ffn_conv_w[l],
                                    row2(ffn_conv_b)[l], zeros_ffn, tm=tm_p, tn=256, seq=tp,
                                    tiles_per_mod=tp // tm_p)
        xp = _ffn_down(f, wd[l], x1, mp[5], row2(g_post_ffn)[l], tm=tm_mix,
                       tiles_per_mod=tp // tm_mix)
        keep_u = jnp.concatenate([keep_a, keep_b], axis=1).reshape(bp, tp // tm_p, SUBLANES, dff2)
        outs_p.append((z[:, COL_KD:COL_KD + w_diff].reshape(bp, tp, H_DIFF, 2 * DK_DIFF),
                       z[:, COL_VD:COL_VD + w_diff].reshape(bp, tp, H_DIFF, DV_DIFF),
                       z[:, COL_KS:COL_KS + w_dsa].reshape(bp, tp, H_DSA, DH_DSA),
                       z[:, COL_VS:COL_VS + w_dsa].reshape(bp, tp, H_DSA, DH_DSA),
                       z[:, COL_SM:COL_SM + D_IDX].reshape(bp, tp, D_IDX),
                       h_last[:, 0], conv_new,
                       keep_u[:, -1, SUBLANES - (FFN_CONV_W - 1):]))

        z = _in_proj(xs, row2(g_pre_mix)[l], ms[1], ms[0], w_in_p[l], rope_s,
                     tm=ns_, tiles_per_mod=1, rope_tiles=1)
        o_lru, h_last, conv_new = _lru(z, state_lru_conv[l], state_lru_h[l][:, None, :], *lru_w,
                                       nb=bs, t=ts, tt=ts)
        kn_i = _pad_rows(z[:, COL_SM:COL_SM + D_IDX], bs, ts)
        sel = _sample_select(z, cik, page_table, tri, kn_i, layer=l, nb=bs, t=ts,
                             pp=min(pp_idx, page_table.shape[1]))
        os_ = _paged_attn(z, csk, csv, page_table, _pad_rows(z[:, COL_KS:COL_KS + w_dsa], bs, ts),
                          _pad_rows(z[:, COL_VS:COL_VS + w_dsa], bs, ts), (sel,), layer=l, nb=bs,
                          t=ts, pp=min(pp_kv, page_table.shape[1]), mode="dsa", q_col=COL_QS)
        kn_d = z[:, COL_KD:COL_KD + w_diff].reshape(ns_ * H_DIFF, 2 * DK_DIFF)
        vn_d = z[:, COL_VD:COL_VD + w_diff].reshape(ns_ * H_DIFF, DV_DIFF)
        od = _paged_attn(z, cdk, cdv, page_table, _pad_rows(kn_d, bs, ts * H_DIFF),
                         _pad_rows(vn_d, bs, ts * H_DIFF), (lp, gsub), layer=l,
                         nb=bs, t=ts, pp=min(pp_kv, page_table.shape[1]), mode="diff", q_col=COL_QD, lam_init=lam_init)
        x1 = _merge(o_lru, od, os_, z, xs, ms[2], wl_, wd_, ws_, wo[l], row2(g_post_mix)[l],
                    tm=ns_, tiles_per_mod=1)
        stf = state_ffn_conv[l]
        st_rows = jnp.stack([
            jnp.pad(stf[:, jj:, :], ((0, 0), (0, ts - (FFN_CONV_W - 1 - jj)), (0, 0))).reshape(ns_, dff2)
            for jj in range(FFN_CONV_W - 1)])
        f, keep_a, keep_b = _ffn_up(x1, row2(g_pre_ffn)[l], ms[4], ms[3], wu[l], ffn_conv_w[l],
                                    row2(ffn_conv_b)[l], st_rows, tm=ns_, tn=256, seq=ts,
                                    tiles_per_mod=1)
        xs = _ffn_down(f, wd[l], x1, ms[5], row2(g_post_ffn)[l], tm=ns_, tiles_per_mod=1)
        u_all = jnp.concatenate([keep_a, keep_b], axis=1).reshape(bs, ts, dff2)
        outs_s.append((z[:, COL_KD:COL_KD + w_diff].reshape(bs, ts, H_DIFF, 2 * DK_DIFF),
                       z[:, COL_VD:COL_VD + w_diff].reshape(bs, ts, H_DIFF, DV_DIFF),
                       z[:, COL_KS:COL_KS + w_dsa].reshape(bs, ts, H_DSA, DH_DSA),
                       z[:, COL_VS:COL_VS + w_dsa].reshape(bs, ts, H_DSA, DH_DSA),
                       z[:, COL_SM:COL_SM + D_IDX].reshape(bs, ts, D_IDX),
                       h_last[:, 0], conv_new,
                       u_all[:, ts - (FFN_CONV_W - 1):]))

    stack = lambda rows: [jnp.stack([r[k] for r in rows]) for k in range(8)]
    return (xp.reshape(bp, tp, d), xs.reshape(bs, ts, d), *stack(outs_p), *stack(outs_s))
```
